```python
import math
import jax, jax.numpy as jnp
from jax import lax
import numpy as np

D_MODEL = 2048
BATCH = 16
SEQ = 2048
DEPTH = 1

GRID_W = 64
CTX_LEN = 256
HG_WIDTH = 1024
HY_WIDTH = D_MODEL - HG_WIDTH
HG_HEAD_DIM = 128
HG_HEADS = HG_WIDTH // HG_HEAD_DIM
HG_ROW_HEADS = HG_HEADS // 2
CHUNK = 64
SHORT_CONV = 3
N_BANDS = 16
FILTER_EMB = 1 + 2 * N_BANDS
FILTER_HIDDEN = 64
FILTER_TARGET = 1e-2
FAST_DECAY_PCT = 0.3
SLOW_DECAY_PCT = 1.5
FILTER_SHIFT = 0.05
D_FF = 4 * D_MODEL
N_MOD = 6
PROJ_WIDTH = 5 * HG_WIDTH + 3 * HY_WIDTH
EPS = 1e-6

kernel_name = "hgrn2_hyena_parallel_prefix_block"


def _rmsnorm(t, g):
    t32 = t.astype(jnp.float32)
    y = t32 * lax.rsqrt(jnp.mean(t32 * t32, axis=-1, keepdims=True) + EPS)
    return (y * g.astype(jnp.float32)).astype(t.dtype)


def _modulate(t, shift, scale):
    return t * (1.0 + scale) + shift


def _col_order(t, rows, inverse):
    b, l, h, d = t.shape
    nc = h - HG_ROW_HEADS
    grid = (GRID_W, rows) if inverse else (rows, GRID_W)
    col = t[:, :, HG_ROW_HEADS:].reshape(b, grid[0], grid[1], nc, d)
    col = jnp.swapaxes(col, 1, 2).reshape(b, l, nc, d)
    return jnp.concatenate([t[:, :, :HG_ROW_HEADS], col], axis=2)


def _chunk_scan(q, k, v, log_f, s0):
    b, l, h, dk = q.shape
    dv = v.shape[-1]
    n = l // CHUNK
    qc = q.reshape(b, n, CHUNK, h, dk)
    kc = k.reshape(b, n, CHUNK, h, dk)
    vc = v.reshape(b, n, CHUNK, h, dv)
    a = jnp.cumsum(log_f.reshape(b, n, CHUNK, h, dk), axis=2)
    a_end = a[:, :, -1:]
    a_mid = a[:, :, CHUNK // 2 - 1:CHUNK // 2]
    scores = jnp.einsum("bnihd,bnjhd->bnhij", qc * jnp.exp(a - a_mid), kc * jnp.exp(a_mid - a))
    tril = jnp.tril(jnp.ones((CHUNK, CHUNK), dtype=bool))
    scores = jnp.where(tril, scores, 0.0)
    o_intra = jnp.einsum("bnhij,bnjhe->bnihe", scores, vc)
    upd = jnp.einsum("bnjhd,bnjhe->bnhde", kc * jnp.exp(a_end - a), vc)
    decay = jnp.exp(a_end[:, :, 0])

    def step(s, xs):
        dec, u = xs
        return dec[..., None] * s + u, s

    s_fin, s_start = lax.scan(step, s0, (jnp.moveaxis(decay, 1, 0), jnp.moveaxis(upd, 1, 0)))
    s_start = jnp.moveaxis(s_start, 0, 1)
    o_inter = jnp.einsum("bnihd,bnhde->bnihe", qc * jnp.exp(a), s_start)
    return (o_intra + o_inter).reshape(b, l, h, dv), s_fin


def _hgrn2_mixer(p_lat, p_ctx, lb, norm_g, rows, need_ctx_out):
    f32 = jnp.float32
    lb_f = lb[0].reshape(HG_HEADS, HG_HEAD_DIM)
    lb_b = lb[1].reshape(HG_HEADS, HG_HEAD_DIM)

    def prep(p):
        p = p.astype(f32)
        b, l, _ = p.shape
        q, zf, zb, v, g = jnp.split(p, 5, axis=-1)
        hd = lambda t: t.reshape(b, l, HG_HEADS, HG_HEAD_DIM)
        f_f = lb_f + (1.0 - lb_f) * jax.nn.sigmoid(hd(zf))
        f_b = lb_b + (1.0 - lb_b) * jax.nn.sigmoid(hd(zb))
        return [hd(q), 1.0 - f_f, jnp.log(f_f), 1.0 - f_b, jnp.log(f_b), hd(v)], g

    (qc, kfc, lffc, kbc, lfbc, vc), gc = prep(p_ctx)
    lat, gl = prep(p_lat)
    ql, kfl, lffl, kbl, lfbl, vl = [_col_order(t, rows, False) for t in lat]
    b = qc.shape[0]
    zeros = jnp.zeros((b, HG_HEADS, HG_HEAD_DIM, HG_HEAD_DIM), f32)
    flip = lambda t: jnp.flip(t, axis=1)
    o_cf, s_cf = _chunk_scan(qc, kfc, vc, lffc, zeros)
    o_cb, s_cb = _chunk_scan(flip(qc), flip(kbc), flip(vc), flip(lfbc), zeros)
    o_lf, _ = _chunk_scan(ql, kfl, vl, lffl, s_cf)
    o_lb, _ = _chunk_scan(flip(ql), flip(kbl), flip(vl), flip(lfbl), s_cb)
    o_lat = _col_order(o_lf + flip(o_lb), rows, True)
    gain = norm_g.astype(f32).reshape(HG_HEADS, HG_HEAD_DIM)

    def readout(o, g):
        bb, l = o.shape[:2]
        o = o * lax.rsqrt(jnp.mean(o * o, axis=-1, keepdims=True) + EPS) * gain
        return o.reshape(bb, l, HG_WIDTH) * jax.nn.silu(g)

    out_ctx = readout(o_cf + flip(o_cb), gc) if need_ctx_out else None
    return readout(o_lat, gl), out_ctx


def _short_conv(t, w, b):
    half = SHORT_CONV // 2
    l = t.shape[1]
    tp = jnp.pad(t, ((0, 0), (half, half), (0, 0)))
    return sum(tp[:, j:j + l] * w[j] for j in range(SHORT_CONV)) + b


def _implicit_filter(l, w1, b1, freq, w2, b2, w3):
    pos = jnp.arange(l, dtype=jnp.float32)[:, None]
    t = pos / max(l - 1, 1)
    bands = jnp.linspace(1e-4, N_BANDS - 1, N_BANDS, dtype=jnp.float32)[None, :]
    ang = bands * (2.0 * math.pi) * pos / l
    z = jnp.concatenate([t, jnp.cos(ang), -jnp.sin(ang)], axis=-1)
    h = jnp.sin(freq * (z @ w1 + b1))
    h = jnp.sin(freq * (h @ w2 + b2))
    h = h @ w3
    deltas = jnp.abs(jnp.linspace(math.log(FILTER_TARGET) / SLOW_DECAY_PCT,
                                  math.log(FILTER_TARGET) / FAST_DECAY_PCT, HY_WIDTH, dtype=jnp.float32))
    deltas = jnp.tile(deltas, 2)
    return h * (jnp.exp(-t * deltas) + FILTER_SHIFT)


def _bidir_long_conv(u, filt, bias):
    l, ch = u.shape[1], u.shape[2]
    kern = jnp.concatenate([filt[:, :ch], jnp.zeros((1, ch), filt.dtype), filt[:0:-1, ch:]], axis=0)
    kern = kern * lax.rsqrt(jnp.sum(kern * kern, axis=0, keepdims=True))
    spec = jnp.fft.rfft(u, n=2 * l, axis=1) * jnp.fft.rfft(kern, axis=0)[None]
    y = jnp.fft.irfft(spec, n=2 * l, axis=1)[:, :l]
    return y + u * bias


def _hyena_mixer(p, conv_w, conv_b, w1, b1, freq, w2, b2, w3, bias):
    f32 = jnp.float32
    p = _short_conv(p.astype(f32), conv_w.astype(f32), conv_b.astype(f32))
    v, x1, x0 = jnp.split(p, 3, axis=-1)
    filt = _implicit_filter(p.shape[1], w1.astype(f32), b1.astype(f32), freq.astype(f32),
                            w2.astype(f32), b2.astype(f32), w3.astype(f32))
    return x0 * _bidir_long_conv(x1 * v, filt, bias.astype(f32))


def _sq_relu_mlp(u, w1, w2):
    return jnp.square(jax.nn.relu(u @ w1)) @ w2


def setup_inputs(seed: int = 0) -> dict:
    key = jax.random.key(seed)
    ks = jax.random.split(key, 24)
    nrm = lambda k, shape, scale: scale * jax.random.normal(k, shape, jnp.float32)
    return {
        "x": nrm(ks[0], (BATCH, SEQ, D_MODEL), 1.0),
        "c": nrm(ks[1], (BATCH, D_MODEL), 1.0),
        "ctx": nrm(ks[2], (BATCH, CTX_LEN, D_MODEL), 1.0),
        "c_ctx": nrm(ks[3], (D_MODEL,), 1.0),
        "w_ada": nrm(ks[4], (DEPTH, D_MODEL, N_MOD * D_MODEL), 0.5 * D_MODEL ** -0.5),
        "b_ada": nrm(ks[5], (DEPTH, N_MOD * D_MODEL), 0.02),
        "norm1_g": 1.0 + nrm(ks[6], (DEPTH, D_MODEL), 0.05),
        "w_in": nrm(ks[7], (DEPTH, D_MODEL, PROJ_WIDTH), D_MODEL ** -0.5),
        "hgrn_lb_logits": nrm(ks[8], (2, DEPTH + 1, HG_WIDTH), 0.5),
        "hgrn_norm_g": 1.0 + nrm(ks[9], (DEPTH, HG_WIDTH), 0.05),
        "hy_conv_w": nrm(ks[10], (DEPTH, SHORT_CONV, 3 * HY_WIDTH), SHORT_CONV ** -0.5),
        "hy_conv_b": nrm(ks[11], (DEPTH, 3 * HY_WIDTH), 0.02),
        "flt_w1": nrm(ks[12], (DEPTH, FILTER_EMB, FILTER_HIDDEN), FILTER_EMB ** -0.5),
        "flt_b1": nrm(ks[13], (DEPTH, FILTER_HIDDEN), 0.1),
        "flt_freq": 1.0 + nrm(ks[14], (DEPTH, FILTER_HIDDEN), 0.05),
        "flt_w2": nrm(ks[15], (DEPTH, FILTER_HIDDEN, FILTER_HIDDEN), FILTER_HIDDEN ** -0.5),
        "flt_b2": nrm(ks[16], (DEPTH, FILTER_HIDDEN), 0.1),
        "flt_w3": nrm(ks[17], (DEPTH, FILTER_HIDDEN, 2 * HY_WIDTH), FILTER_HIDDEN ** -0.5),
        "hy_bias": nrm(ks[18], (DEPTH, HY_WIDTH), 0.5),
        "w_out": nrm(ks[19], (DEPTH, D_MODEL, D_MODEL), D_MODEL ** -0.5),
        "norm2_g": 1.0 + nrm(ks[20], (DEPTH, D_MODEL), 0.05),
        "w_mlp1": nrm(ks[21], (DEPTH, D_MODEL, D_FF), D_MODEL ** -0.5),
        "w_mlp2": nrm(ks[22], (DEPTH, D_FF, D_MODEL), D_FF ** -0.5),
        "final_norm_g": 1.0 + nrm(ks[23], (D_MODEL,), 0.05),
    }


def reference(x, c, ctx, c_ctx, w_ada, b_ada, norm1_g, w_in, hgrn_lb_logits, hgrn_norm_g,
              hy_conv_w, hy_conv_b, flt_w1, flt_b1, flt_freq, flt_w2, flt_b2, flt_w3, hy_bias,
              w_out, norm2_g, w_mlp1, w_mlp2, final_norm_g):
    rows = x.shape[1] // GRID_W
    hg_cols = 5 * HG_WIDTH
    lb_all = jnp.cumsum(jax.nn.softmax(hgrn_lb_logits.astype(jnp.float32), axis=1), axis=1)
    silu_c = jax.nn.silu(c)
    silu_cc = jax.nn.silu(c_ctx)
    for layer in range(DEPTH):
        need_ctx = layer + 1 < DEPTH
        m_lat = (silu_c @ w_ada[layer] + b_ada[layer])[:, None, :]
        m_ctx = silu_cc @ w_ada[layer] + b_ada[layer]
        sh1, sc1, g1, sh2, sc2, g2 = jnp.split(m_lat, N_MOD, axis=-1)
        csh1, csc1, cg1, csh2, csc2, cg2 = jnp.split(m_ctx, N_MOD, axis=-1)

        u_lat = _modulate(_rmsnorm(x, norm1_g[layer]), sh1, sc1)
        u_ctx = _modulate(_rmsnorm(ctx, norm1_g[layer]), csh1, csc1)
        p_lat = u_lat @ w_in[layer]
        p_ctx = u_ctx @ w_in[layer]

        a_lat, a_ctx = _hgrn2_mixer(p_lat[..., :hg_cols], p_ctx[..., :hg_cols], lb_all[:, layer],
                                    hgrn_norm_g[layer], rows, need_ctx)
        hy_args = (hy_conv_w[layer], hy_conv_b[layer], flt_w1[layer], flt_b1[layer], flt_freq[layer],
                   flt_w2[layer], flt_b2[layer], flt_w3[layer], hy_bias[layer])
        y_lat = _hyena_mixer(p_lat[..., hg_cols:], *hy_args)

        mix_lat = jnp.concatenate([a_lat, y_lat], axis=-1).astype(x.dtype) @ w_out[layer]
        x = x + g1 * mix_lat
        x = x + g2 * _sq_relu_mlp(_modulate(_rmsnorm(x, norm2_g[layer]), sh2, sc2), w_mlp1[layer], w_mlp2[layer])

        if need_ctx:
            y_ctx = _hyena_mixer(p_ctx[..., hg_cols:], *hy_args)
            mix_ctx = jnp.concatenate([a_ctx, y_ctx], axis=-1).astype(ctx.dtype) @ w_out[layer]
            ctx = ctx + cg1 * mix_ctx
            ctx = ctx + cg2 * _sq_relu_mlp(_modulate(_rmsnorm(ctx, norm2_g[layer]), csh2, csc2),
                                           w_mlp1[layer], w_mlp2[layer])
    return _rmsnorm(x, final_norm_g)
```

```python
import functools
import math

import jax
import jax.numpy as jnp
from jax import lax
from jax.experimental import pallas as pl
from jax.experimental.pallas import tpu as pltpu

GRID_W = 64
HG_WIDTH = 1024
HG_HEAD_DIM = 128
HG_HEADS = HG_WIDTH // HG_HEAD_DIM
HG_ROW_HEADS = HG_HEADS // 2
CHUNK = 64
N_BANDS = 16
FILTER_TARGET = 1e-2
FAST_DECAY_PCT = 0.3
SLOW_DECAY_PCT = 1.5
FILTER_SHIFT = 0.05
N_MOD = 6
EPS = 1e-6

F32 = jnp.float32
BF16 = jnp.bfloat16

V7X_VMEM_BYTES = 64 * 1024 * 1024
VMEM_LIMIT = 56 * 1024 * 1024
LANES = 128


def _cparams(sem):
    return pltpu.CompilerParams(dimension_semantics=sem, vmem_limit_bytes=VMEM_LIMIT)


def _split_bf16(a):
    hi = a.astype(BF16)
    lo = (a - hi.astype(F32)).astype(BF16)
    return hi, lo


def _dot(a, b):
    return jnp.dot(a, b, preferred_element_type=F32)


def _dot3(a, b):
    ah, al = _split_bf16(a)
    bh, bl = _split_bf16(b)
    return _dot(ah, bh) + _dot(ah, bl) + _dot(al, bh)


def _rms(x, g):
    return x * lax.rsqrt(jnp.mean(x * x, axis=-1, keepdims=True) + EPS) * g


def _ada_kernel(s_ref, w_ref, b_ref, o_ref):
    s = s_ref[...]
    s = s * jax.nn.sigmoid(s)
    o_ref[...] = _dot3(s, w_ref[...]) + b_ref[...]


def _ada(stacked, w, b):
    r, d = stacked.shape
    n = w.shape[1]
    tn = 512
    return pl.pallas_call(
        _ada_kernel,
        grid=(n // tn,),
        in_specs=[pl.BlockSpec((r, d), lambda j: (0, 0)),
                  pl.BlockSpec((d, tn), lambda j: (0, j)),
                  pl.BlockSpec((1, tn), lambda j: (0, j))],
        out_specs=pl.BlockSpec((r, tn), lambda j: (0, j)),
        out_shape=jax.ShapeDtypeStruct((r, n), F32),
        compiler_params=_cparams(("parallel",)),
        name="ada_modulation",
    )(stacked, w, b.reshape(1, n))


def _nmm_kernel(x_ref, g_ref, sh_ref, sc_ref, w_ref, o_ref, u_ref):
    @pl.when(pl.program_id(1) == 0)
    def _():
        y = _rms(x_ref[...], g_ref[...])
        u_ref[...] = (y * (1.0 + sc_ref[...]) + sh_ref[...]).astype(BF16)

    o_ref[...] = _dot(u_ref[...], w_ref[...]).astype(o_ref.dtype)


def _norm_mod_matmul(x2, g, shift, scale, w, rows_per_mod, out_dtype, tm, tn, name):
    t, d = x2.shape
    n = w.shape[1]
    m = shift.shape[0]
    per = rows_per_mod // tm
    mod_map = lambda i, j: (i // per, 0, 0)
    return pl.pallas_call(
        _nmm_kernel,
        grid=(t // tm, n // tn),
        in_specs=[pl.BlockSpec((tm, d), lambda i, j: (i, 0)),
                  pl.BlockSpec((1, d), lambda i, j: (0, 0)),
                  pl.BlockSpec((None, 1, d), mod_map),
                  pl.BlockSpec((None, 1, d), mod_map),
                  pl.BlockSpec((d, tn), lambda i, j: (0, j))],
        out_specs=pl.BlockSpec((tm, tn), lambda i, j: (i, j)),
        out_shape=jax.ShapeDtypeStruct((t, n), out_dtype),
        scratch_shapes=[pltpu.VMEM((tm, d), BF16)],
        compiler_params=_cparams(("parallel", "arbitrary")),
        name=name,
    )(x2, g.reshape(1, d), shift.reshape(m, 1, d), scale.reshape(m, 1, d), w)


def _hgrn_chunk(q, z, v, lb, st, fwd, tri, mask):
    f = lb + (1.0 - lb) * jax.nn.sigmoid(z)
    k = 1.0 - f
    lf_hi, lf_lo = _split_bf16(jnp.log(f))
    a = _dot(tri, lf_hi) + _dot(tri, lf_lo)
    mid = CHUNK // 2 - 1 if fwd else CHUNK // 2
    end = CHUNK - 1 if fwd else 0
    a_mid = a[mid:mid + 1]
    a_end = a[end:end + 1]
    vb = v.astype(BF16)
    kd = (k * jnp.exp(a_end - a)).astype(BF16)
    upd = lax.dot_general(vb, kd, (((0,), (0,)), ((), ())), preferred_element_type=F32)
    st_new = st * jnp.exp(a_end) + upd
    if q is None:
        return None, st_new
    qs = (q * jnp.exp(a - a_mid)).astype(BF16)
    ks = (k * jnp.exp(a_mid - a)).astype(BF16)
    s = lax.dot_general(qs, ks, (((1,), (1,)), ((), ())), preferred_element_type=F32)
    s = jnp.where(mask, s, 0.0).astype(BF16)
    qa = (q * jnp.exp(a)).astype(BF16)
    o = _dot(s, vb) + lax.dot_general(qa, st.astype(BF16), (((1,), (1,)), ((), ())),
                                      preferred_element_type=F32)
    return o, st_new


def _hgrn_kernel(*refs, heads, col_order, rows):
    n_lat_refs, n_ctx_refs = 5, 3
    lat = [refs[h * n_lat_refs:(h + 1) * n_lat_refs] for h in range(heads)]
    base = heads * n_lat_refs
    ctx = [refs[base + h * n_ctx_refs:base + (h + 1) * n_ctx_refs] for h in range(heads)]
    lbl_ref, gain_ref, o_ref, acc_ref, st_ref = refs[base + heads * n_ctx_refs:]
    seq = lat[0][0].shape[0]
    ctx_len = ctx[0][0].shape[0]
    n_lat = seq // CHUNK
    n_ctx = ctx_len // CHUNK
    n_slots = lbl_ref.shape[0] // 2
    cols_per_chunk = CHUNK // rows

    ii = lax.broadcasted_iota(jnp.int32, (CHUNK, CHUNK), 0)
    jj = lax.broadcasted_iota(jnp.int32, (CHUNK, CHUNK), 1)
    lower = ii >= jj
    upper = ii <= jj
    tri_f = lower.astype(BF16)
    tri_b = upper.astype(BF16)

    def lower_bound(direction):
        lg = lbl_ref[direction * n_slots:(direction + 1) * n_slots, :]
        e = jnp.exp(lg - jnp.max(lg, axis=0, keepdims=True))
        return e[0:1] / jnp.sum(e, axis=0, keepdims=True)

    lb_f = lower_bound(0)
    lb_b = lower_bound(1)
    gain = gain_ref[...]

    def lat_rows(ref, n):
        if not col_order:
            return ref[pl.ds(pl.multiple_of(n * CHUNK, CHUNK), CHUNK), :]
        parts = [ref[pl.ds(n * cols_per_chunk + c, rows, stride=GRID_W), :] for c in range(cols_per_chunk)]
        return jnp.concatenate(parts, axis=0)

    def ctx_rows(ref, n):
        return ref[pl.ds(pl.multiple_of(n * CHUNK, CHUNK), CHUNK), :]

    def hsl(h):
        return slice(h * HG_HEAD_DIM, (h + 1) * HG_HEAD_DIM)

    def run(fwd):
        lb = lb_f if fwd else lb_b
        zi = 0 if fwd else 1
        tri, mask = (tri_f, lower) if fwd else (tri_b, upper)
        st_ref[...] = jnp.zeros_like(st_ref)

        def ctx_body(i, carry):
            n = i if fwd else n_ctx - 1 - i
            for h in range(heads):
                z = ctx_rows(ctx[h][zi], n)
                v = ctx_rows(ctx[h][2], n)
                _, st_new = _hgrn_chunk(None, z, v, lb[:, hsl(h)], st_ref[h], fwd, tri, mask)
                st_ref[h] = st_new
            return carry

        lax.fori_loop(0, n_ctx, ctx_body, 0)

        def lat_body(i, carry):
            n = i if fwd else n_lat - 1 - i
            dst = pl.ds(pl.multiple_of(n * CHUNK, CHUNK), CHUNK)
            for h in range(heads):
                q_ref, zf_ref, zb_ref, v_ref, g_ref = lat[h]
                q = lat_rows(q_ref, n)
                z = lat_rows(zf_ref if fwd else zb_ref, n)
                v = lat_rows(v_ref, n)
                o, st_new = _hgrn_chunk(q, z, v, lb[:, hsl(h)], st_ref[h], fwd, tri, mask)
                st_ref[h] = st_new
                if fwd:
                    acc_ref[h, dst, :] = o
                else:
                    o = o + acc_ref[h, dst, :]
                    o = o * lax.rsqrt(jnp.mean(o * o, axis=-1, keepdims=True) + EPS) * gain[:, hsl(h)]
                    g = lat_rows(g_ref, n)
                    acc_ref[h, dst, :] = o * (g * jax.nn.sigmoid(g))
            return carry

        lax.fori_loop(0, n_lat, lat_body, 0)

    run(True)
    run(False)

    for h in range(heads):
        if not col_order:
            o_ref[:, hsl(h)] = acc_ref[h].astype(o_ref.dtype)
        else:
            def unpermute(r, carry, h=h):
                blk = acc_ref.at[h][pl.ds(r, GRID_W, stride=rows), :]
                o_ref[pl.ds(pl.multiple_of(r * GRID_W, GRID_W), GRID_W), hsl(h)] = blk.astype(o_ref.dtype)
                return carry

            lax.fori_loop(0, rows, unpermute, 0)


def _hgrn(p_hg, p_ctx, lb_logits, gain, head0, n_heads, col_order, rows, name):
    b, seq, _ = p_hg.shape
    ctx_len = p_ctx.shape[1]
    hps = 2
    width = hps * HG_HEAD_DIM
    groups = n_heads // hps
    g0 = head0 // hps
    n_rows = lb_logits.shape[0]

    def slab(length, part, h):
        return pl.BlockSpec((None, length, HG_HEAD_DIM),
                            lambda bi, gi: (bi, 0, part * HG_HEADS + head0 + gi * hps + h))

    lat_specs = [slab(seq, part, h) for h in range(hps) for part in range(5)]
    ctx_specs = [slab(ctx_len, part, h) for h in range(hps) for part in range(3)]
    kern = functools.partial(_hgrn_kernel, heads=hps, col_order=col_order, rows=rows)
    return pl.pallas_call(
        kern,
        grid=(b, groups),
        in_specs=lat_specs + ctx_specs + [
            pl.BlockSpec((n_rows, width), lambda bi, gi: (0, g0 + gi)),
            pl.BlockSpec((1, width), lambda bi, gi: (0, g0 + gi))],
        out_specs=pl.BlockSpec((None, seq, width), lambda bi, gi: (bi, 0, gi)),
        out_shape=jax.ShapeDtypeStruct((b, seq, n_heads * HG_HEAD_DIM), BF16),
        scratch_shapes=[pltpu.VMEM((hps, seq, HG_HEAD_DIM), F32),
                        pltpu.VMEM((hps, HG_HEAD_DIM, HG_HEAD_DIM), F32)],
        compiler_params=_cparams(("parallel", "parallel")),
        name=name,
    )(*([p_hg] * len(lat_specs) + [p_ctx] * len(ctx_specs) + [lb_logits, gain]))


def _dft_tables(seq):
    n = 2 * seq
    j = lax.broadcasted_iota(jnp.int32, (n, seq), 0)
    t = lax.broadcasted_iota(jnp.int32, (n, seq), 1)
    f = jnp.where(j < seq, j, j - seq)
    ang = ((f * t) % n).astype(F32) * (2.0 * math.pi / n)
    nyq = jnp.where(t % 2 == 0, 1.0, -1.0)
    tab = jnp.where(j < seq, jnp.cos(ang), jnp.where(j == seq, nyq, jnp.sin(ang)))
    return tab.astype(BF16), tab.T.astype(BF16)


def _filter_embedding(seq):
    pos = jnp.arange(seq, dtype=F32)[:, None]
    t = pos / max(seq - 1, 1)
    bands = jnp.linspace(1e-4, N_BANDS - 1, N_BANDS, dtype=F32)[None, :]
    ang = bands * (2.0 * math.pi) * pos / seq
    return jnp.concatenate([t, jnp.cos(ang), -jnp.sin(ang)], axis=-1)


def _filter_deltas(width):
    return jnp.abs(jnp.linspace(math.log(FILTER_TARGET) / SLOW_DECAY_PCT,
                                math.log(FILTER_TARGET) / FAST_DECAY_PCT, width, dtype=F32))


def _kspec_kernel(z_ref, w1_ref, b1_ref, fr_ref, w2_ref, b2_ref, w3f_ref, w3b_ref, dl_ref, f_ref, k_ref):
    seq = z_ref.shape[0]
    ct = k_ref.shape[1]
    fr = fr_ref[...]
    h = jnp.sin(fr * (_dot3(z_ref[...], w1_ref[...]) + b1_ref[...]))
    h = jnp.sin(fr * (_dot3(h, w2_ref[...]) + b2_ref[...]))
    row = lax.broadcasted_iota(jnp.int32, (seq, ct), 0)
    t = row.astype(F32) / max(seq - 1, 1)
    window = jnp.exp(-t * dl_ref[...]) + FILTER_SHIFT
    hf = _dot3(h, w3f_ref[...]) * window
    hb = jnp.where(row == 0, 0.0, _dot3(h, w3b_ref[...]) * window)
    r = lax.rsqrt(jnp.sum(hf * hf, axis=0, keepdims=True) + jnp.sum(hb * hb, axis=0, keepdims=True))
    fmat = f_ref[...]
    kf = _dot(fmat, (hf * r).astype(BF16))
    kb = _dot(fmat, (hb * r).astype(BF16))
    jrow = lax.broadcasted_iota(jnp.int32, (2 * seq, ct), 0)
    k_ref[...] = kf + jnp.where(jrow > seq, -kb, kb)


def _kspec(z, w1, b1, freq, w2, b2, w3, deltas, fmat, ct):
    seq = z.shape[0]
    ch = deltas.shape[1]
    hid = w2.shape[0]
    nb = ch // ct
    full = lambda a: pl.BlockSpec(a.shape, lambda c: (0,) * a.ndim)
    return pl.pallas_call(
        _kspec_kernel,
        grid=(nb,),
        in_specs=[full(z), full(w1), full(b1), full(freq), full(w2), full(b2),
                  pl.BlockSpec((hid, ct), lambda c: (0, c)),
                  pl.BlockSpec((hid, ct), lambda c: (0, nb + c)),
                  pl.BlockSpec((1, ct), lambda c: (0, c)),
                  pl.BlockSpec(fmat.shape, lambda c: (0, 0), pipeline_mode=pl.Buffered(1))],
        out_specs=pl.BlockSpec((2 * seq, ct), lambda c: (0, c)),
        out_shape=jax.ShapeDtypeStruct((2 * seq, ch), F32),
        compiler_params=_cparams(("arbitrary",)),
        name="hyena_filter_spectrum",
    )(z, w1, b1, freq, w2, b2, w3, w3, deltas, fmat)


def _short_conv(t, w, b):
    seq = t.shape[0]
    row = lax.broadcasted_iota(jnp.int32, t.shape, 0)
    prev = jnp.where(row == 0, 0.0, pltpu.roll(t, 1, 0))
    nxt = jnp.where(row == seq - 1, 0.0, pltpu.roll(t, seq - 1, 0))
    return prev * w[0:1] + t * w[1:2] + nxt * w[2:3] + b


def _hy_fwd_kernel(v_ref, x1_ref, wv_ref, wx1_ref, bv_ref, bx1_ref, f_ref, k_ref, z_ref):
    seq = v_ref.shape[0]
    n = 2 * seq
    vc = _short_conv(v_ref[...].astype(F32), wv_ref[...], bv_ref[...])
    x1c = _short_conv(x1_ref[...].astype(F32), wx1_ref[...], bx1_ref[...])
    ub = (x1c * vc).astype(BF16)
    xa = _dot(f_ref[0:seq, :], ub)
    xb = _dot(f_ref[seq:n, :], ub)
    ka = k_ref[0:seq, :]
    kb = k_ref[seq:n, :]
    first = lax.broadcasted_iota(jnp.int32, xa.shape, 0) == 0
    w = jnp.where(first, 1.0 / n, 2.0 / n)
    bb = xb * kb
    z_ref[0:seq, :] = ((xa * ka - jnp.where(first, 0.0, bb)) * w).astype(z_ref.dtype)
    z_ref[seq:n, :] = (jnp.where(first, bb, xa * kb + xb * ka) * w).astype(z_ref.dtype)


def _hy_inv_kernel(z_ref, g_ref, v_ref, x1_ref, x0_ref, wv_ref, wx1_ref, wx0_ref, bv_ref, bx1_ref, bx0_ref,
                   bias_ref, o_ref):
    vc = _short_conv(v_ref[...].astype(F32), wv_ref[...], bv_ref[...])
    x1c = _short_conv(x1_ref[...].astype(F32), wx1_ref[...], bx1_ref[...])
    x0c = _short_conv(x0_ref[...].astype(F32), wx0_ref[...], bx0_ref[...])
    u = x1c * vc
    y = _dot(g_ref[...], z_ref[...])
    o_ref[...] = (x0c * (y + u * bias_ref[...])).astype(o_ref.dtype)


def _hyena(p_hy, conv_w, conv_b, kspec, bias, fmat, gmat, ct):
    b, seq, _ = p_hy.shape
    ch = bias.shape[1]
    nb = ch // ct
    taps = conv_w.shape[0]
    act = lambda part: pl.BlockSpec((None, seq, ct), lambda c, bi: (bi, 0, part * nb + c))
    cw = lambda part: pl.BlockSpec((taps, ct), lambda c, bi: (0, part * nb + c))
    cb = lambda part: pl.BlockSpec((1, ct), lambda c, bi: (0, part * nb + c))
    const = lambda a: pl.BlockSpec(a.shape, lambda c, bi: (0, 0), pipeline_mode=pl.Buffered(1))
    zspec = pl.BlockSpec((None, 2 * seq, ct), lambda c, bi: (bi, 0, c))
    z = pl.pallas_call(
        _hy_fwd_kernel,
        grid=(nb, b),
        in_specs=[act(0), act(1), cw(0), cw(1), cb(0), cb(1), const(fmat),
                  pl.BlockSpec((2 * seq, ct), lambda c, bi: (0, c))],
        out_specs=zspec,
        out_shape=jax.ShapeDtypeStruct((b, 2 * seq, ch), BF16),
        compiler_params=_cparams(("parallel", "parallel")),
        name="hyena_forward_dft",
    )(p_hy, p_hy, conv_w, conv_w, conv_b, conv_b, fmat, kspec)
    return pl.pallas_call(
        _hy_inv_kernel,
        grid=(nb, b),
        in_specs=[zspec, const(gmat), act(0), act(1), act(2), cw(0), cw(1), cw(2), cb(0), cb(1), cb(2),
                  pl.BlockSpec((1, ct), lambda c, bi: (0, c))],
        out_specs=pl.BlockSpec((None, seq, ct), lambda c, bi: (bi, 0, c)),
        out_shape=jax.ShapeDtypeStruct((b, seq, ch), BF16),
        compiler_params=_cparams(("parallel", "parallel")),
        name="hyena_inverse_dft",
    )(z, gmat, p_hy, p_hy, p_hy, conv_w, conv_w, conv_w, conv_b, conv_b, conv_b, bias)


def _wout_kernel(ar_ref, ac_ref, y_ref, x_ref, gate_ref, w_ref, o_ref):
    mix = jnp.concatenate([ar_ref[...], ac_ref[...], y_ref[...]], axis=1)
    o_ref[...] = x_ref[...] + gate_ref[...] * _dot(mix, w_ref[...])


def _wout(a_row, a_col, y, x2, gate, w, rows_per_mod, tm):
    t, d = x2.shape
    m = gate.shape[0]
    per = rows_per_mod // tm
    wr, wc, wy = a_row.shape[1], a_col.shape[1], y.shape[1]
    return pl.pallas_call(
        _wout_kernel,
        grid=(t // tm,),
        in_specs=[pl.BlockSpec((tm, wr), lambda i: (i, 0)),
                  pl.BlockSpec((tm, wc), lambda i: (i, 0)),
                  pl.BlockSpec((tm, wy), lambda i: (i, 0)),
                  pl.BlockSpec((tm, d), lambda i: (i, 0)),
                  pl.BlockSpec((None, 1, d), lambda i: (i // per, 0, 0)),
                  pl.BlockSpec(w.shape, lambda i: (0, 0))],
        out_specs=pl.BlockSpec((tm, d), lambda i: (i, 0)),
        out_shape=jax.ShapeDtypeStruct((t, d), F32),
        compiler_params=_cparams(("parallel",)),
        name="out_projection_residual",
    )(a_row, a_col, y, x2, gate.reshape(m, 1, d), w)


def _mlp_kernel(x_ref, g_ref, sh_ref, sc_ref, gate_ref, w1_ref, w2_ref, fg_ref, o_ref, u_ref, acc_ref):
    j = pl.program_id(1)

    @pl.when(j == 0)
    def _():
        y = _rms(x_ref[...], g_ref[...])
        u_ref[...] = (y * (1.0 + sc_ref[...]) + sh_ref[...]).astype(BF16)

    h = jnp.maximum(_dot(u_ref[...], w1_ref[...]), 0.0)
    part = _dot((h * h).astype(BF16), w2_ref[...])

    @pl.when(j == 0)
    def _():
        acc_ref[...] = part

    @pl.when(j > 0)
    def _():
        acc_ref[...] += part

    @pl.when(j == pl.num_programs(1) - 1)
    def _():
        o_ref[...] = _rms(x_ref[...] + gate_ref[...] * acc_ref[...], fg_ref[...])


def _mlp(x2, g, shift, scale, gate, w1, w2, fg, rows_per_mod, tm, tf):
    t, d = x2.shape
    dff = w1.shape[1]
    m = shift.shape[0]
    per = rows_per_mod // tm
    mod = pl.BlockSpec((None, 1, d), lambda i, j: (i // per, 0, 0))
    vec = pl.BlockSpec((1, d), lambda i, j: (0, 0))
    r3 = lambda a: a.reshape(m, 1, d)
    return pl.pallas_call(
        _mlp_kernel,
        grid=(t // tm, dff // tf),
        in_specs=[pl.BlockSpec((tm, d), lambda i, j: (i, 0)), vec, mod, mod, mod,
                  pl.BlockSpec((d, tf), lambda i, j: (0, j)),
                  pl.BlockSpec((tf, d), lambda i, j: (j, 0)),
                  vec],
        out_specs=pl.BlockSpec((tm, d), lambda i, j: (i, 0)),
        out_shape=jax.ShapeDtypeStruct((t, d), F32),
        scratch_shapes=[pltpu.VMEM((tm, d), BF16), pltpu.VMEM((tm, d), F32)],
        compiler_params=_cparams(("parallel", "arbitrary")),
        name="mlp_residual_final_norm",
    )(x2, g.reshape(1, d), r3(shift), r3(scale), r3(gate), w1, w2, fg.reshape(1, d))


def kernel(x, c, ctx, c_ctx, w_ada, b_ada, norm1_g, w_in, hgrn_lb_logits, hgrn_norm_g, hy_conv_w, hy_conv_b,
           flt_w1, flt_b1, flt_freq, flt_w2, flt_b2, flt_w3, hy_bias, w_out, norm2_g, w_mlp1, w_mlp2,
           final_norm_g):
    b, seq, d = x.shape
    ctx_len = ctx.shape[1]
    depth = w_ada.shape[0]
    assert depth == 1, "single-layer block"
    rows = seq // GRID_W
    hg_cols = 5 * HG_WIDTH
    hy_width = d - HG_WIDTH
    layer = 0

    pad = (-(b + 1)) % 8
    stacked = jnp.concatenate([c, c_ctx[None, :], jnp.zeros((pad, d), F32)], axis=0)
    mod = _ada(stacked, w_ada[layer], b_ada[layer])
    sh1, sc1, g1, sh2, sc2, g2 = [mod[:b, i * d:(i + 1) * d] for i in range(N_MOD)]
    csh1, csc1 = mod[b:b + 1, 0:d], mod[b:b + 1, d:2 * d]

    w_in_b = w_in[layer].astype(BF16)
    x2 = x.reshape(b * seq, d)
    ctx2 = ctx.reshape(b * ctx_len, d)
    p_hg = _norm_mod_matmul(x2, norm1_g[layer], sh1, sc1, w_in_b[:, :hg_cols], seq, F32, 1024, 1024,
                            "in_projection_hgrn").reshape(b, seq, hg_cols)
    p_hy = _norm_mod_matmul(x2, norm1_g[layer], sh1, sc1, w_in_b[:, hg_cols:], seq, BF16, 1024, 1024,
                            "in_projection_hyena").reshape(b, seq, 3 * hy_width)
    p_ctx = _norm_mod_matmul(ctx2, norm1_g[layer], csh1, csc1, w_in_b[:, HG_WIDTH:4 * HG_WIDTH], b * ctx_len,
                             F32, min(1024, b * ctx_len), 1024, "in_projection_context").reshape(b, ctx_len, 3 * HG_WIDTH)

    lbl = hgrn_lb_logits.astype(F32).reshape(2 * (depth + 1), HG_WIDTH)
    gain = hgrn_norm_g[layer].reshape(1, HG_WIDTH)
    a_row = _hgrn(p_hg, p_ctx, lbl, gain, 0, HG_ROW_HEADS, False, rows, "hgrn_row_heads")
    a_col = _hgrn(p_hg, p_ctx, lbl, gain, HG_ROW_HEADS, HG_HEADS - HG_ROW_HEADS, True, rows, "hgrn_col_heads")

    fmat, gmat = _dft_tables(seq)
    emb = _filter_embedding(seq)
    kpad = (-emb.shape[1]) % LANES
    emb = jnp.pad(emb, ((0, 0), (0, kpad)))
    fw1 = jnp.pad(flt_w1[layer], ((0, kpad), (0, 0)))
    row = lambda a: a[layer].reshape(1, -1)
    kspec = _kspec(emb, fw1, row(flt_b1), row(flt_freq), flt_w2[layer], row(flt_b2), flt_w3[layer],
                   _filter_deltas(hy_width)[None, :], fmat, 256)
    y_lat = _hyena(p_hy, hy_conv_w[layer], hy_conv_b[layer].reshape(1, -1), kspec, row(hy_bias), fmat, gmat, 256)

    x_mid = _wout(a_row.reshape(b * seq, -1), a_col.reshape(b * seq, -1), y_lat.reshape(b * seq, hy_width),
                  x2, g1, w_out[layer].astype(BF16), seq, 512)
    out = _mlp(x_mid, norm2_g[layer], sh2, sc2, g2, w_mlp1[layer].astype(BF16), w_mlp2[layer].astype(BF16),
               final_norm_g, seq, 512, 1024)
    return out.reshape(b, seq, d)
```

```python
import functools
import math

import jax
import jax.numpy as jnp
from jax import lax
from jax.experimental import pallas as pl
from jax.experimental.pallas import tpu as pltpu

GRID_W = 64
HG_WIDTH = 1024
HG_HEAD_DIM = 128
HG_HEADS = HG_WIDTH // HG_HEAD_DIM
HG_ROW_HEADS = HG_HEADS // 2
CHUNK = 64
N_BANDS = 16
FILTER_TARGET = 1e-2
FAST_DECAY_PCT = 0.3
SLOW_DECAY_PCT = 1.5
FILTER_SHIFT = 0.05
N_MOD = 6
EPS = 1e-6

F32 = jnp.float32
BF16 = jnp.bfloat16

V7X_VMEM_BYTES = 64 * 1024 * 1024
VMEM_LIMIT = 56 * 1024 * 1024
LANES = 128
SUBLANES = 8

SUPER = 256
N_SUB = SUPER // CHUNK


def _cparams(sem):
    return pltpu.CompilerParams(dimension_semantics=sem, vmem_limit_bytes=VMEM_LIMIT)


def _split_bf16(a):
    hi = a.astype(BF16)
    lo = (a - hi.astype(F32)).astype(BF16)
    return hi, lo


def _dot(a, b):
    return jnp.dot(a, b, preferred_element_type=F32)


def _dot_nt(a, b):
    return lax.dot_general(a, b, (((1,), (1,)), ((), ())), preferred_element_type=F32)


def _dot_tn(a, b):
    return lax.dot_general(a, b, (((0,), (0,)), ((), ())), preferred_element_type=F32)


def _dot3(a, b):
    ah, al = _split_bf16(a)
    bh, bl = _split_bf16(b)
    return _dot(ah, bh) + _dot(ah, bl) + _dot(al, bh)


def _rms(x, g):
    return x * lax.rsqrt(jnp.mean(x * x, axis=-1, keepdims=True) + EPS) * g


def _ilog2(n):
    assert n > 0 and n & (n - 1) == 0, n
    return n.bit_length() - 1


def _ada_kernel(s_ref, w_ref, b_ref, o_ref):
    s = s_ref[...]
    s = s * jax.nn.sigmoid(s)
    o_ref[...] = _dot3(s, w_ref[...]) + b_ref[...]


def _ada(stacked, w, b):
    r, d = stacked.shape
    n = w.shape[1]
    tn = 512
    return pl.pallas_call(
        _ada_kernel,
        grid=(n // tn,),
        in_specs=[pl.BlockSpec((r, d), lambda j: (0, 0)),
                  pl.BlockSpec((d, tn), lambda j: (0, j)),
                  pl.BlockSpec((1, tn), lambda j: (0, j))],
        out_specs=pl.BlockSpec((r, tn), lambda j: (0, j)),
        out_shape=jax.ShapeDtypeStruct((r, n), F32),
        compiler_params=_cparams(("parallel",)),
        name="ada_modulation",
    )(stacked, w, b.reshape(1, n))


def _nmm_kernel(x_ref, g_ref, sh_ref, sc_ref, w_ref, o_ref, u_ref):
    @pl.when(pl.program_id(1) == 0)
    def _():
        y = _rms(x_ref[...], g_ref[...])
        u_ref[...] = (y * (1.0 + sc_ref[...]) + sh_ref[...]).astype(BF16)

    o_ref[...] = _dot(u_ref[...], w_ref[...]).astype(o_ref.dtype)


def _norm_mod_matmul(x2, g, shift, scale, w, rows_per_mod, out_dtype, tm, tn, name):
    t, d = x2.shape
    n = w.shape[1]
    m = shift.shape[0]
    per = rows_per_mod // tm
    mod_map = lambda i, j: (i // per, 0, 0)
    return pl.pallas_call(
        _nmm_kernel,
        grid=(t // tm, n // tn),
        in_specs=[pl.BlockSpec((tm, d), lambda i, j: (i, 0)),
                  pl.BlockSpec((1, d), lambda i, j: (0, 0)),
                  pl.BlockSpec((None, 1, d), mod_map),
                  pl.BlockSpec((None, 1, d), mod_map),
                  pl.BlockSpec((d, tn), lambda i, j: (0, j))],
        out_specs=pl.BlockSpec((tm, tn), lambda i, j: (i, j)),
        out_shape=jax.ShapeDtypeStruct((t, n), out_dtype),
        scratch_shapes=[pltpu.VMEM((tm, d), BF16)],
        compiler_params=_cparams(("parallel", "arbitrary")),
        name=name,
    )(x2, g.reshape(1, d), shift.reshape(m, 1, d), scale.reshape(m, 1, d), w)


def _scan_pos(t, col_order, rows):
    if not col_order:
        return t
    cols = SUPER // rows
    return jnp.bitwise_and(t, cols - 1) * rows + jnp.right_shift(t, _ilog2(cols))


def _fill_scan_constants(dm_ref, mask_ref, rsel_ref, cm_ref, slot, col_order, rows):
    lc = _ilog2(CHUNK)
    half = CHUNK // 2
    pi = _scan_pos(lax.broadcasted_iota(jnp.int32, (SUPER, SUPER), 0), col_order, rows)
    pj = _scan_pos(lax.broadcasted_iota(jnp.int32, (SUPER, SUPER), 1), col_order, rows)
    same = jnp.right_shift(pi, lc) == jnp.right_shift(pj, lc)
    wj = jnp.bitwise_and(pj, CHUNK - 1)
    c8 = lax.broadcasted_iota(jnp.int32, (2 * N_SUB, SUPER), 0)
    p8 = _scan_pos(lax.broadcasted_iota(jnp.int32, (2 * N_SUB, SUPER), 1), col_order, rows)
    in_chunk = jnp.right_shift(p8, lc) == jnp.bitwise_and(c8, N_SUB - 1)
    w8 = jnp.bitwise_and(p8, CHUNK - 1)
    for d, fwd in enumerate((True, False)):
        incl = same & ((pj <= pi) if fwd else (pj >= pi))
        upto_mid = same & ((wj < half) if fwd else (wj >= half))
        dm_ref[2 * slot + d] = (incl.astype(F32) - upto_mid.astype(F32)).astype(BF16)
        mask_ref[2 * slot + d] = incl.astype(F32)
        sel = in_chunk & ((c8 >= N_SUB) | ((w8 < half) if fwd else (w8 >= half)))
        rsel_ref[2 * slot + d] = sel.astype(F32).astype(BF16)
    pr = jnp.right_shift(_scan_pos(lax.broadcasted_iota(jnp.int32, (SUPER, HG_HEAD_DIM), 0), col_order, rows), lc)
    for c in range(N_SUB):
        cm_ref[slot, c] = (pr == c).astype(F32).astype(BF16)


def _hgrn_kernel(q_ref, zf_ref, zb_ref, v_ref, g_ref, czf_ref, czb_ref, cv_ref, lbl_ref, gain_ref, o_ref,
                 acc_ref, st_ref, dm_ref, mask_ref, rsel_ref, cm_ref, *, col_order, rows):
    seq, width = q_ref.shape
    ctx_len = cv_ref.shape[0]
    heads = width // HG_HEAD_DIM
    n_lat = seq // SUPER
    n_ctx = ctx_len // SUPER
    n_slots = lbl_ref.shape[0] // 2
    cols = SUPER // rows
    d = HG_HEAD_DIM

    orders = (False, True) if col_order else (False,)
    for slot, order in enumerate(orders):
        _fill_scan_constants(dm_ref, mask_ref, rsel_ref, cm_ref, slot, order, rows)
    lat_slot = len(orders) - 1

    def lower_bound(direction):
        lg = lbl_ref[direction * n_slots:(direction + 1) * n_slots, :]
        e = jnp.exp(lg - jnp.max(lg, axis=0, keepdims=True))
        return e[0:1] / jnp.sum(e, axis=0, keepdims=True)

    lbs = (lower_bound(0), lower_bound(1))
    gain = gain_ref[...]

    def row_starts(sc, order):
        if not order:
            return [(pl.multiple_of(sc * SUPER, SUPER), SUPER)]
        return [(pl.multiple_of(r * GRID_W + sc * cols, cols), cols) for r in range(rows)]

    def load_rows(ref, sc, order):
        parts = [ref[pl.ds(s, n), :] for s, n in row_starts(sc, order)]
        return parts[0] if len(parts) == 1 else jnp.concatenate(parts, axis=0)

    def store_rows(ref, sc, order, val):
        off = 0
        for s, n in row_starts(sc, order):
            ref[pl.ds(s, n), :] = val[off:off + n]
            off += n

    def hsl(h):
        return slice(h * d, (h + 1) * d)

    def scan_step(sc, di, slot, order, q_r, z_r, v_r):
        fwd = di == 0
        k_idx = 2 * slot + di
        z = load_rows(z_r, sc, order)
        lb = lbs[di]
        f = lb + (1.0 - lb) * jax.nn.sigmoid(z)
        k = 1.0 - f
        lf_hi, lf_lo = _split_bf16(jnp.log(f))
        dm = dm_ref[k_idx]
        x1 = _dot(dm, lf_hi) + _dot(dm, lf_lo)
        rsel = rsel_ref[k_idx]
        ref_sums = _dot(rsel, lf_hi) + _dot(rsel, lf_lo)
        a_mid = ref_sums[0:N_SUB]
        a_end = ref_sums[N_SUB:2 * N_SUB]
        upd_scale = jnp.exp(a_end - a_mid)
        decay = jnp.exp(a_end)
        mid_scale = jnp.exp(a_mid)
        ks = (k * jnp.exp(-x1)).astype(BF16)
        vb = load_rows(v_r, sc, order).astype(BF16)
        if q_r is not None:
            qs = (load_rows(q_r, sc, order) * jnp.exp(x1)).astype(BF16)
            allowed = mask_ref[k_idx] != 0.0
        outs = []
        for h in range(heads):
            sl = hsl(h)
            ks_h = ks[:, sl]
            vb_h = vb[:, sl]
            ks_sub = jnp.concatenate([ks_h * cm_ref[slot, c] for c in range(N_SUB)], axis=1)
            upd = _dot_tn(vb_h, ks_sub)
            st = st_ref[h]
            entering = [None] * N_SUB
            for c in (range(N_SUB) if fwd else reversed(range(N_SUB))):
                entering[c] = st * mid_scale[c:c + 1, sl]
                st = st * decay[c:c + 1, sl] + upd[:, c * d:(c + 1) * d] * upd_scale[c:c + 1, sl]
            st_ref[h] = st
            if q_r is not None:
                qs_h = qs[:, sl]
                s = jnp.where(allowed, _dot_nt(qs_h, ks_h), 0.0).astype(BF16)
                qs_sub = jnp.concatenate([qs_h * cm_ref[slot, c] for c in range(N_SUB)], axis=1)
                st_cat = jnp.concatenate(entering, axis=1).astype(BF16)
                outs.append(_dot(s, vb_h) + _dot_nt(qs_sub, st_cat))
        return jnp.concatenate(outs, axis=1) if outs else None

    def run(di):
        fwd = di == 0
        z_r, cz_r = (zf_ref, czf_ref) if fwd else (zb_ref, czb_ref)
        st_ref[...] = jnp.zeros_like(st_ref)

        def ctx_body(i, carry):
            scan_step(i if fwd else n_ctx - 1 - i, di, 0, False, None, cz_r, cv_ref)
            return carry

        lax.fori_loop(0, n_ctx, ctx_body, 0)

        def lat_body(i, carry):
            sc = i if fwd else n_lat - 1 - i
            o = scan_step(sc, di, lat_slot, col_order, q_ref, z_r, v_ref)
            if not fwd:
                o = o + load_rows(acc_ref, sc, col_order)
                g = load_rows(g_ref, sc, col_order)
                res = []
                for h in range(heads):
                    oh = o[:, hsl(h)]
                    res.append(oh * lax.rsqrt(jnp.mean(oh * oh, axis=-1, keepdims=True) + EPS))
                o = jnp.concatenate(res, axis=1) * gain * (g * jax.nn.sigmoid(g))
            store_rows(acc_ref, sc, col_order, o)
            return carry

        lax.fori_loop(0, n_lat, lat_body, 0, unroll=2)

    run(0)
    run(1)
    o_ref[...] = acc_ref[...].astype(o_ref.dtype)


def _hgrn(p_hg, p_ctx, lb_logits, gain, head0, n_heads, col_order, rows, name):
    b, seq, _ = p_hg.shape
    ctx_len = p_ctx.shape[1]
    assert seq % SUPER == 0 and ctx_len % SUPER == 0 and 2 * N_SUB == SUBLANES
    assert not col_order or (rows * SUBLANES == SUPER and GRID_W % SUBLANES == 0)
    hps = 2
    width = hps * HG_HEAD_DIM
    groups = n_heads // hps
    part_blocks = HG_WIDTH // width
    g0 = head0 // hps
    n_rows = lb_logits.shape[0]
    n_orders = 2 if col_order else 1

    def slab(length, part):
        return pl.BlockSpec((None, length, width), lambda bi, gi: (bi, 0, part * part_blocks + g0 + gi))

    kern = functools.partial(_hgrn_kernel, col_order=col_order, rows=rows)
    return pl.pallas_call(
        kern,
        grid=(b, groups),
        in_specs=[slab(seq, 0), slab(seq, 1), slab(seq, 2), slab(seq, 3), slab(seq, 4),
                  slab(ctx_len, 0), slab(ctx_len, 1), slab(ctx_len, 2),
                  pl.BlockSpec((n_rows, width), lambda bi, gi: (0, g0 + gi)),
                  pl.BlockSpec((1, width), lambda bi, gi: (0, g0 + gi))],
        out_specs=pl.BlockSpec((None, seq, width), lambda bi, gi: (bi, 0, gi)),
        out_shape=jax.ShapeDtypeStruct((b, seq, n_heads * HG_HEAD_DIM), BF16),
        scratch_shapes=[pltpu.VMEM((seq, width), F32),
                        pltpu.VMEM((hps, HG_HEAD_DIM, HG_HEAD_DIM), F32),
                        pltpu.VMEM((2 * n_orders, SUPER, SUPER), BF16),
                        pltpu.VMEM((2 * n_orders, SUPER, SUPER), F32),
                        pltpu.VMEM((2 * n_orders, 2 * N_SUB, SUPER), BF16),
                        pltpu.VMEM((n_orders, N_SUB, SUPER, HG_HEAD_DIM), BF16)],
        compiler_params=_cparams(("parallel", "parallel")),
        name=name,
    )(p_hg, p_hg, p_hg, p_hg, p_hg, p_ctx, p_ctx, p_ctx, lb_logits, gain)


def _dft_tables(seq):
    n = 2 * seq
    lo = GRID_W
    j = jnp.arange(n, dtype=jnp.int32)[:, None]
    f = jnp.where(j < seq, j, j - seq)
    ang = lambda m: (m % n).astype(F32) * (2.0 * math.pi / n)
    a = ang(f * (jnp.arange(seq // lo, dtype=jnp.int32) * lo)[None, :])[:, :, None]
    b = ang(f * jnp.arange(lo, dtype=jnp.int32)[None, :])[:, None, :]
    ca, sa, cb, sb = jnp.cos(a), jnp.sin(a), jnp.cos(b), jnp.sin(b)
    cos_ft = (ca * cb - sa * sb).reshape(n, seq)
    sin_ft = (sa * cb + ca * sb).reshape(n, seq)
    t = jnp.arange(seq, dtype=jnp.int32)[None, :]
    nyq = jnp.where(t % 2 == 0, 1.0, -1.0)
    tab = jnp.where(j < seq, cos_ft, jnp.where(j == seq, nyq, sin_ft)).astype(BF16)
    tab = lax.optimization_barrier(tab)
    return tab, tab.T


def _filter_embedding(seq):
    pos = jnp.arange(seq, dtype=F32)[:, None]
    t = pos / max(seq - 1, 1)
    bands = jnp.linspace(1e-4, N_BANDS - 1, N_BANDS, dtype=F32)[None, :]
    ang = bands * (2.0 * math.pi) * pos / seq
    return jnp.concatenate([t, jnp.cos(ang), -jnp.sin(ang)], axis=-1)


def _filter_deltas(width):
    return jnp.abs(jnp.linspace(math.log(FILTER_TARGET) / SLOW_DECAY_PCT,
                                math.log(FILTER_TARGET) / FAST_DECAY_PCT, width, dtype=F32))


def _kspec_kernel(z_ref, w1_ref, b1_ref, fr_ref, w2_ref, b2_ref, w3f_ref, w3b_ref, dl_ref, f_ref, k_ref):
    seq = z_ref.shape[0]
    ct = k_ref.shape[1]
    fr = fr_ref[...]
    h = jnp.sin(fr * (_dot3(z_ref[...], w1_ref[...]) + b1_ref[...]))
    h = jnp.sin(fr * (_dot3(h, w2_ref[...]) + b2_ref[...]))
    row = lax.broadcasted_iota(jnp.int32, (seq, ct), 0)
    t = row.astype(F32) / max(seq - 1, 1)
    window = jnp.exp(-t * dl_ref[...]) + FILTER_SHIFT
    hf = _dot3(h, w3f_ref[...]) * window
    hb = jnp.where(row == 0, 0.0, _dot3(h, w3b_ref[...]) * window)
    r = lax.rsqrt(jnp.sum(hf * hf, axis=0, keepdims=True) + jnp.sum(hb * hb, axis=0, keepdims=True))
    fmat = f_ref[...]
    kf = _dot(fmat, (hf * r).astype(BF16))
    kb = _dot(fmat, (hb * r).astype(BF16))
    jrow = lax.broadcasted_iota(jnp.int32, (2 * seq, ct), 0)
    k_ref[...] = kf + jnp.where(jrow > seq, -kb, kb)


def _kspec(z, w1, b1, freq, w2, b2, w3, deltas, fmat, ct):
    seq = z.shape[0]
    ch = deltas.shape[1]
    hid = w2.shape[0]
    nb = ch // ct
    full = lambda a: pl.BlockSpec(a.shape, lambda c: (0,) * a.ndim)
    return pl.pallas_call(
        _kspec_kernel,
        grid=(nb,),
        in_specs=[full(z), full(w1), full(b1), full(freq), full(w2), full(b2),
                  pl.BlockSpec((hid, ct), lambda c: (0, c)),
                  pl.BlockSpec((hid, ct), lambda c: (0, nb + c)),
                  pl.BlockSpec((1, ct), lambda c: (0, c)),
                  pl.BlockSpec(fmat.shape, lambda c: (0, 0), pipeline_mode=pl.Buffered(1))],
        out_specs=pl.BlockSpec((2 * seq, ct), lambda c: (0, c)),
        out_shape=jax.ShapeDtypeStruct((2 * seq, ch), F32),
        compiler_params=_cparams(("arbitrary",)),
        name="hyena_filter_spectrum",
    )(z, w1, b1, freq, w2, b2, w3, w3, deltas, fmat)


def _short_conv(t, w, b):
    seq = t.shape[0]
    row = lax.broadcasted_iota(jnp.int32, t.shape, 0)
    prev = jnp.where(row == 0, 0.0, pltpu.roll(t, 1, 0))
    nxt = jnp.where(row == seq - 1, 0.0, pltpu.roll(t, seq - 1, 0))
    return prev * w[0:1] + t * w[1:2] + nxt * w[2:3] + b


def _hy_fwd_kernel(v_ref, x1_ref, wv_ref, wx1_ref, bv_ref, bx1_ref, f_ref, k_ref, z_ref):
    seq = v_ref.shape[0]
    n = 2 * seq
    vc = _short_conv(v_ref[...].astype(F32), wv_ref[...], bv_ref[...])
    x1c = _short_conv(x1_ref[...].astype(F32), wx1_ref[...], bx1_ref[...])
    ub = (x1c * vc).astype(BF16)
    xa = _dot(f_ref[0:seq, :], ub)
    xb = _dot(f_ref[seq:n, :], ub)
    ka = k_ref[0:seq, :]
    kb = k_ref[seq:n, :]
    first = lax.broadcasted_iota(jnp.int32, xa.shape, 0) == 0
    w = jnp.where(first, 1.0 / n, 2.0 / n)
    bb = xb * kb
    z_ref[0:seq, :] = ((xa * ka - jnp.where(first, 0.0, bb)) * w).astype(z_ref.dtype)
    z_ref[seq:n, :] = (jnp.where(first, bb, xa * kb + xb * ka) * w).astype(z_ref.dtype)


def _hy_inv_kernel(z_ref, g_ref, v_ref, x1_ref, x0_ref, wv_ref, wx1_ref, wx0_ref, bv_ref, bx1_ref, bx0_ref,
                   bias_ref, o_ref):
    vc = _short_conv(v_ref[...].astype(F32), wv_ref[...], bv_ref[...])
    x1c = _short_conv(x1_ref[...].astype(F32), wx1_ref[...], bx1_ref[...])
    x0c = _short_conv(x0_ref[...].astype(F32), wx0_ref[...], bx0_ref[...])
    u = x1c * vc
    y = _dot(g_ref[...], z_ref[...])
    o_ref[...] = (x0c * (y + u * bias_ref[...])).astype(o_ref.dtype)


def _hyena(p_hy, conv_w, conv_b, kspec, bias, fmat, gmat, ct):
    b, seq, _ = p_hy.shape
    ch = bias.shape[1]
    nb = ch // ct
    taps = conv_w.shape[0]
    act = lambda part: pl.BlockSpec((None, seq, ct), lambda c, bi: (bi, 0, part * nb + c))
    cw = lambda part: pl.BlockSpec((taps, ct), lambda c, bi: (0, part * nb + c))
    cb = lambda part: pl.BlockSpec((1, ct), lambda c, bi: (0, part * nb + c))
    const = lambda a: pl.BlockSpec(a.shape, lambda c, bi: (0, 0), pipeline_mode=pl.Buffered(1))
    zspec = pl.BlockSpec((None, 2 * seq, ct), lambda c, bi: (bi, 0, c))
    z = pl.pallas_call(
        _hy_fwd_kernel,
        grid=(nb, b),
        in_specs=[act(0), act(1), cw(0), cw(1), cb(0), cb(1), const(fmat),
                  pl.BlockSpec((2 * seq, ct), lambda c, bi: (0, c))],
        out_specs=zspec,
        out_shape=jax.ShapeDtypeStruct((b, 2 * seq, ch), BF16),
        compiler_params=_cparams(("parallel", "parallel")),
        name="hyena_forward_dft",
    )(p_hy, p_hy, conv_w, conv_w, conv_b, conv_b, fmat, kspec)
    return pl.pallas_call(
        _hy_inv_kernel,
        grid=(nb, b),
        in_specs=[zspec, const(gmat), act(0), act(1), act(2), cw(0), cw(1), cw(2), cb(0), cb(1), cb(2),
                  pl.BlockSpec((1, ct), lambda c, bi: (0, c))],
        out_specs=pl.BlockSpec((None, seq, ct), lambda c, bi: (bi, 0, c)),
        out_shape=jax.ShapeDtypeStruct((b, seq, ch), BF16),
        compiler_params=_cparams(("parallel", "parallel")),
        name="hyena_inverse_dft",
    )(z, gmat, p_hy, p_hy, p_hy, conv_w, conv_w, conv_w, conv_b, conv_b, conv_b, bias)


def _wout_kernel(ar_ref, ac_ref, y_ref, x_ref, gate_ref, w_ref, o_ref):
    mix = jnp.concatenate([ar_ref[...], ac_ref[...], y_ref[...]], axis=1)
    o_ref[...] = x_ref[...] + gate_ref[...] * _dot(mix, w_ref[...])


def _wout(a_row, a_col, y, x2, gate, w, rows_per_mod, tm):
    t, d = x2.shape
    m = gate.shape[0]
    per = rows_per_mod // tm
    wr, wc, wy = a_row.shape[1], a_col.shape[1], y.shape[1]
    return pl.pallas_call(
        _wout_kernel,
        grid=(t // tm,),
        in_specs=[pl.BlockSpec((tm, wr), lambda i: (i, 0)),
                  pl.BlockSpec((tm, wc), lambda i: (i, 0)),
                  pl.BlockSpec((tm, wy), lambda i: (i, 0)),
                  pl.BlockSpec((tm, d), lambda i: (i, 0)),
                  pl.BlockSpec((None, 1, d), lambda i: (i // per, 0, 0)),
                  pl.BlockSpec(w.shape, lambda i: (0, 0))],
        out_specs=pl.BlockSpec((tm, d), lambda i: (i, 0)),
        out_shape=jax.ShapeDtypeStruct((t, d), F32),
        compiler_params=_cparams(("parallel",)),
        name="out_projection_residual",
    )(a_row, a_col, y, x2, gate.reshape(m, 1, d), w)


def _mlp_kernel(x_ref, g_ref, sh_ref, sc_ref, gate_ref, w1_ref, w2_ref, fg_ref, o_ref, u_ref, acc_ref):
    j = pl.program_id(1)

    @pl.when(j == 0)
    def _():
        y = _rms(x_ref[...], g_ref[...])
        u_ref[...] = (y * (1.0 + sc_ref[...]) + sh_ref[...]).astype(BF16)

    h = jnp.maximum(_dot(u_ref[...], w1_ref[...]), 0.0)
    part = _dot((h * h).astype(BF16), w2_ref[...])

    @pl.when(j == 0)
    def _():
        acc_ref[...] = part

    @pl.when(j > 0)
    def _():
        acc_ref[...] += part

    @pl.when(j == pl.num_programs(1) - 1)
    def _():
        o_ref[...] = _rms(x_ref[...] + gate_ref[...] * acc_ref[...], fg_ref[...])


def _mlp(x2, g, shift, scale, gate, w1, w2, fg, rows_per_mod, tm, tf):
    t, d = x2.shape
    dff = w1.shape[1]
    m = shift.shape[0]
    per = rows_per_mod // tm
    mod = pl.BlockSpec((None, 1, d), lambda i, j: (i // per, 0, 0))
    vec = pl.BlockSpec((1, d), lambda i, j: (0, 0))
    r3 = lambda a: a.reshape(m, 1, d)
    return pl.pallas_call(
        _mlp_kernel,
        grid=(t // tm, dff // tf),
        in_specs=[pl.BlockSpec((tm, d), lambda i, j: (i, 0)), vec, mod, mod, mod,
                  pl.BlockSpec((d, tf), lambda i, j: (0, j)),
                  pl.BlockSpec((tf, d), lambda i, j: (j, 0)),
                  vec],
        out_specs=pl.BlockSpec((tm, d), lambda i, j: (i, 0)),
        out_shape=jax.ShapeDtypeStruct((t, d), F32),
        scratch_shapes=[pltpu.VMEM((tm, d), BF16), pltpu.VMEM((tm, d), F32)],
        compiler_params=_cparams(("parallel", "arbitrary")),
        name="mlp_residual_final_norm",
    )(x2, g.reshape(1, d), r3(shift), r3(scale), r3(gate), w1, w2, fg.reshape(1, d))


def kernel(x, c, ctx, c_ctx, w_ada, b_ada, norm1_g, w_in, hgrn_lb_logits, hgrn_norm_g, hy_conv_w, hy_conv_b,
           flt_w1, flt_b1, flt_freq, flt_w2, flt_b2, flt_w3, hy_bias, w_out, norm2_g, w_mlp1, w_mlp2,
           final_norm_g):
    b, seq, d = x.shape
    ctx_len = ctx.shape[1]
    depth = w_ada.shape[0]
    assert depth == 1, "single-layer block"
    rows = seq // GRID_W
    hg_cols = 5 * HG_WIDTH
    hy_width = d - HG_WIDTH
    layer = 0

    pad = (-(b + 1)) % SUBLANES
    stacked = jnp.concatenate([c, c_ctx[None, :], jnp.zeros((pad, d), F32)], axis=0)
    mod = _ada(stacked, w_ada[layer], b_ada[layer])
    sh1, sc1, g1, sh2, sc2, g2 = [mod[:b, i * d:(i + 1) * d] for i in range(N_MOD)]
    csh1, csc1 = mod[b:b + 1, 0:d], mod[b:b + 1, d:2 * d]

    w_in_b = w_in[layer].astype(BF16)
    x2 = x.reshape(b * seq, d)
    ctx2 = ctx.reshape(b * ctx_len, d)
    p_hg = _norm_mod_matmul(x2, norm1_g[layer], sh1, sc1, w_in_b[:, :hg_cols], seq, F32, 1024, 1024,
                            "in_projection_hgrn").reshape(b, seq, hg_cols)
    p_hy = _norm_mod_matmul(x2, norm1_g[layer], sh1, sc1, w_in_b[:, hg_cols:], seq, BF16, 1024, 1024,
                            "in_projection_hyena").reshape(b, seq, 3 * hy_width)
    p_ctx = _norm_mod_matmul(ctx2, norm1_g[layer], csh1, csc1, w_in_b[:, HG_WIDTH:4 * HG_WIDTH], b * ctx_len,
                             F32, min(1024, b * ctx_len), 1024, "in_projection_context")
    p_ctx = p_ctx.reshape(b, ctx_len, 3 * HG_WIDTH)

    lbl = hgrn_lb_logits.astype(F32).reshape(2 * (depth + 1), HG_WIDTH)
    gain = hgrn_norm_g[layer].reshape(1, HG_WIDTH)
    a_row = _hgrn(p_hg, p_ctx, lbl, gain, 0, HG_ROW_HEADS, False, rows, "hgrn_row_heads")
    a_col = _hgrn(p_hg, p_ctx, lbl, gain, HG_ROW_HEADS, HG_HEADS - HG_ROW_HEADS, True, rows, "hgrn_col_heads")

    fmat, gmat = _dft_tables(seq)
    emb = _filter_embedding(seq)
    kpad = (-emb.shape[1]) % LANES
    emb = jnp.pad(emb, ((0, 0), (0, kpad)))
    fw1 = jnp.pad(flt_w1[layer], ((0, kpad), (0, 0)))
    row = lambda a: a[layer].reshape(1, -1)
    kspec = _kspec(emb, fw1, row(flt_b1), row(flt_freq), flt_w2[layer], row(flt_b2), flt_w3[layer],
                   _filter_deltas(hy_width)[None, :], fmat, 256)
    y_lat = _hyena(p_hy, hy_conv_w[layer], hy_conv_b[layer].reshape(1, -1), kspec, row(hy_bias), fmat, gmat, 256)

    x_mid = _wout(a_row.reshape(b * seq, -1), a_col.reshape(b * seq, -1), y_lat.reshape(b * seq, hy_width),
                  x2, g1, w_out[layer].astype(BF16), seq, 512)
    out = _mlp(x_mid, norm2_g[layer], sh2, sc2, g2, w_mlp1[layer].astype(BF16), w_mlp2[layer].astype(BF16),
               final_norm_g, seq, 512, 1024)
    return out.reshape(b, seq, d)
```

```python
import functools
import math

import jax
import jax.numpy as jnp
from jax import lax
from jax.experimental import pallas as pl
from jax.experimental.pallas import tpu as pltpu

GRID_W = 64
HG_WIDTH = 1024
HG_HEAD_DIM = 128
HG_HEADS = HG_WIDTH // HG_HEAD_DIM
HG_ROW_HEADS = HG_HEADS // 2
CHUNK = 64
N_BANDS = 16
FILTER_TARGET = 1e-2
FAST_DECAY_PCT = 0.3
SLOW_DECAY_PCT = 1.5
FILTER_SHIFT = 0.05
N_MOD = 6
EPS = 1e-6

F32 = jnp.float32
BF16 = jnp.bfloat16

V7X_VMEM_BYTES = 64 * 1024 * 1024
VMEM_LIMIT = 56 * 1024 * 1024
LANES = 128
SUBLANES = 8

SUPER = 256
N_SUB = SUPER // CHUNK

MLP_ROW_SPLIT = 2


def _cparams(sem):
    return pltpu.CompilerParams(dimension_semantics=sem, vmem_limit_bytes=VMEM_LIMIT)


def _split_bf16(a):
    hi = a.astype(BF16)
    lo = (a - hi.astype(F32)).astype(BF16)
    return hi, lo


def _dot(a, b):
    return jnp.dot(a, b, preferred_element_type=F32)


def _dot_nt(a, b):
    return lax.dot_general(a, b, (((1,), (1,)), ((), ())), preferred_element_type=F32)


def _dot_tn(a, b):
    return lax.dot_general(a, b, (((0,), (0,)), ((), ())), preferred_element_type=F32)


def _dot3(a, b):
    ah, al = _split_bf16(a)
    bh, bl = _split_bf16(b)
    return _dot(ah, bh) + _dot(ah, bl) + _dot(al, bh)


def _rms(x, g):
    return x * lax.rsqrt(jnp.mean(x * x, axis=-1, keepdims=True) + EPS) * g


def _ilog2(n):
    assert n > 0 and n & (n - 1) == 0, n
    return n.bit_length() - 1


def _ada_kernel(s_ref, w_ref, b_ref, o_ref):
    s = s_ref[...]
    s = s * jax.nn.sigmoid(s)
    o_ref[...] = _dot3(s, w_ref[...]) + b_ref[...]


def _ada(stacked, w, b):
    r, d = stacked.shape
    n = w.shape[1]
    tn = 512
    return pl.pallas_call(
        _ada_kernel,
        grid=(n // tn,),
        in_specs=[pl.BlockSpec((r, d), lambda j: (0, 0)),
                  pl.BlockSpec((d, tn), lambda j: (0, j)),
                  pl.BlockSpec((1, tn), lambda j: (0, j))],
        out_specs=pl.BlockSpec((r, tn), lambda j: (0, j)),
        out_shape=jax.ShapeDtypeStruct((r, n), F32),
        compiler_params=_cparams(("parallel",)),
        name="ada_modulation",
    )(stacked, w, b.reshape(1, n))


def _nmm_kernel(x_ref, g_ref, sh_ref, sc_ref, w_ref, o_ref, u_ref):
    @pl.when(pl.program_id(1) == 0)
    def _():
        y = _rms(x_ref[...], g_ref[...])
        u_ref[...] = (y * (1.0 + sc_ref[...]) + sh_ref[...]).astype(BF16)

    o_ref[...] = _dot(u_ref[...], w_ref[...]).astype(o_ref.dtype)


def _norm_mod_matmul(x2, g, shift, scale, w, rows_per_mod, out_dtype, tm, tn, name):
    t, d = x2.shape
    n = w.shape[1]
    m = shift.shape[0]
    per = rows_per_mod // tm
    mod_map = lambda i, j: (i // per, 0, 0)
    return pl.pallas_call(
        _nmm_kernel,
        grid=(t // tm, n // tn),
        in_specs=[pl.BlockSpec((tm, d), lambda i, j: (i, 0)),
                  pl.BlockSpec((1, d), lambda i, j: (0, 0)),
                  pl.BlockSpec((None, 1, d), mod_map),
                  pl.BlockSpec((None, 1, d), mod_map),
                  pl.BlockSpec((d, tn), lambda i, j: (0, j))],
        out_specs=pl.BlockSpec((tm, tn), lambda i, j: (i, j)),
        out_shape=jax.ShapeDtypeStruct((t, n), out_dtype),
        scratch_shapes=[pltpu.VMEM((tm, d), BF16)],
        compiler_params=_cparams(("parallel", "arbitrary")),
        name=name,
    )(x2, g.reshape(1, d), shift.reshape(m, 1, d), scale.reshape(m, 1, d), w)


def _nmm_pipelined_kernel(xs_ref, g_ref, sh_ref, sc_ref, w_ref, o_ref, u_ref):
    i = pl.program_id(0)
    j = pl.program_id(1)
    ts = xs_ref.shape[0]
    slot = i % 2

    @pl.when((i == 0) & (j == 0))
    def _():
        u_ref[1] = jnp.zeros(u_ref.shape[1:], u_ref.dtype)

    y = _rms(xs_ref[...], g_ref[...])
    u_ref[slot, pl.ds(pl.multiple_of(j * ts, ts), ts), :] = (y * (1.0 + sc_ref[...]) + sh_ref[...]).astype(BF16)
    o_ref[...] = _dot(u_ref[1 - slot], w_ref[...])


def _norm_mod_matmul_pipelined(x2, g, shift, scale, w, rows_per_mod, tm, tn, name):
    t, d = x2.shape
    n = w.shape[1]
    m = shift.shape[0]
    per = rows_per_mod // tm
    nt, nc = t // tm, n // tn
    ts = tm // nc
    assert ts * nc == tm and ts % SUBLANES == 0
    cur = lambda i: jnp.minimum(i, nt - 1)
    mod_map = lambda i, j: (cur(i) // per, 0, 0)
    return pl.pallas_call(
        _nmm_pipelined_kernel,
        grid=(nt + 1, nc),
        in_specs=[pl.BlockSpec((ts, d), lambda i, j: (cur(i) * nc + j, 0)),
                  pl.BlockSpec((1, d), lambda i, j: (0, 0)),
                  pl.BlockSpec((None, 1, d), mod_map),
                  pl.BlockSpec((None, 1, d), mod_map),
                  pl.BlockSpec((d, tn), lambda i, j: (0, j))],
        out_specs=pl.BlockSpec((tm, tn), lambda i, j: (jnp.maximum(i - 1, 0), jnp.where(i == 0, 0, j))),
        out_shape=jax.ShapeDtypeStruct((t, n), F32),
        scratch_shapes=[pltpu.VMEM((2, tm, d), BF16)],
        compiler_params=_cparams(("arbitrary", "arbitrary")),
        name=name,
    )(x2, g.reshape(1, d), shift.reshape(m, 1, d), scale.reshape(m, 1, d), w)


def _scan_pos(t, col_order, rows):
    if not col_order:
        return t
    cols = SUPER // rows
    return jnp.bitwise_and(t, cols - 1) * rows + jnp.right_shift(t, _ilog2(cols))


def _scan_constants(orders, rows):
    lc = _ilog2(CHUNK)
    half = CHUNK // 2
    dms, masks, rsels, cms = [], [], [], []
    for col_order in orders:
        pi = _scan_pos(lax.broadcasted_iota(jnp.int32, (SUPER, SUPER), 0), col_order, rows)
        pj = _scan_pos(lax.broadcasted_iota(jnp.int32, (SUPER, SUPER), 1), col_order, rows)
        same = jnp.right_shift(pi, lc) == jnp.right_shift(pj, lc)
        wj = jnp.bitwise_and(pj, CHUNK - 1)
        c8 = lax.broadcasted_iota(jnp.int32, (2 * N_SUB, SUPER), 0)
        p8 = _scan_pos(lax.broadcasted_iota(jnp.int32, (2 * N_SUB, SUPER), 1), col_order, rows)
        in_chunk = jnp.right_shift(p8, lc) == jnp.bitwise_and(c8, N_SUB - 1)
        w8 = jnp.bitwise_and(p8, CHUNK - 1)
        for fwd in (True, False):
            incl = same & ((pj <= pi) if fwd else (pj >= pi))
            upto_mid = same & ((wj < half) if fwd else (wj >= half))
            dms.append((incl.astype(F32) - upto_mid.astype(F32)).astype(BF16))
            masks.append(incl.astype(F32))
            sel = in_chunk & ((c8 >= N_SUB) | ((w8 < half) if fwd else (w8 >= half)))
            rsels.append(sel.astype(BF16))
        pr = jnp.right_shift(
            _scan_pos(lax.broadcasted_iota(jnp.int32, (SUPER, HG_HEAD_DIM), 0), col_order, rows), lc)
        cms.append(jnp.stack([(pr == c).astype(BF16) for c in range(N_SUB)]))
    return jnp.stack(dms), jnp.stack(masks), jnp.stack(rsels), jnp.stack(cms)


def _hgrn_kernel(q_ref, zf_ref, zb_ref, v_ref, g_ref, czf_ref, czb_ref, cv_ref, lbl_ref, gain_ref,
                 dm_ref, mask_ref, rsel_ref, cm_ref, o_ref, acc_ref, st_ref, *, col_order, rows):
    seq, width = q_ref.shape
    ctx_len = cv_ref.shape[0]
    heads = width // HG_HEAD_DIM
    n_lat = seq // SUPER
    n_ctx = ctx_len // SUPER
    n_slots = lbl_ref.shape[0] // 2
    cols = SUPER // rows
    d = HG_HEAD_DIM
    lat_slot = cm_ref.shape[0] - 1

    def lower_bound(direction):
        lg = lbl_ref[direction * n_slots:(direction + 1) * n_slots, :]
        e = jnp.exp(lg - jnp.max(lg, axis=0, keepdims=True))
        return e[0:1] / jnp.sum(e, axis=0, keepdims=True)

    lbs = (lower_bound(0), lower_bound(1))
    gain = gain_ref[...]

    def row_starts(sc, order):
        if not order:
            return [(pl.multiple_of(sc * SUPER, SUPER), SUPER)]
        return [(pl.multiple_of(r * GRID_W + sc * cols, cols), cols) for r in range(rows)]

    def load_rows(ref, sc, order):
        parts = [ref[pl.ds(s, n), :] for s, n in row_starts(sc, order)]
        return parts[0] if len(parts) == 1 else jnp.concatenate(parts, axis=0)

    def store_rows(ref, sc, order, val):
        off = 0
        for s, n in row_starts(sc, order):
            ref[pl.ds(s, n), :] = val[off:off + n]
            off += n

    def hsl(h):
        return slice(h * d, (h + 1) * d)

    def scan_step(sc, di, slot, order, q_r, z_r, v_r):
        fwd = di == 0
        k_idx = 2 * slot + di
        z = load_rows(z_r, sc, order)
        lb = lbs[di]
        f = lb + (1.0 - lb) * jax.nn.sigmoid(z)
        k = 1.0 - f
        lf_hi, lf_lo = _split_bf16(jnp.log(f))
        dm = dm_ref[k_idx]
        x1 = _dot(dm, lf_hi) + _dot(dm, lf_lo)
        rsel = rsel_ref[k_idx]
        ref_sums = _dot(rsel, lf_hi) + _dot(rsel, lf_lo)
        a_mid = ref_sums[0:N_SUB]
        a_end = ref_sums[N_SUB:2 * N_SUB]
        upd_scale = jnp.exp(a_end - a_mid)
        decay = jnp.exp(a_end)
        mid_scale = jnp.exp(a_mid)
        ks = (k * jnp.exp(-x1)).astype(BF16)
        vb = load_rows(v_r, sc, order).astype(BF16)
        if q_r is not None:
            qs = (load_rows(q_r, sc, order) * jnp.exp(x1)).astype(BF16)
            allowed = mask_ref[k_idx] != 0.0
        outs = []
        for h in range(heads):
            sl = hsl(h)
            ks_h = ks[:, sl]
            vb_h = vb[:, sl]
            ks_sub = jnp.concatenate([ks_h * cm_ref[slot, c] for c in range(N_SUB)], axis=1)
            upd = _dot_tn(vb_h, ks_sub)
            st = st_ref[di * heads + h]
            entering = [None] * N_SUB
            for c in (range(N_SUB) if fwd else reversed(range(N_SUB))):
                entering[c] = st * mid_scale[c:c + 1, sl]
                st = st * decay[c:c + 1, sl] + upd[:, c * d:(c + 1) * d] * upd_scale[c:c + 1, sl]
            st_ref[di * heads + h] = st
            if q_r is not None:
                qs_h = qs[:, sl]
                s = jnp.where(allowed, _dot_nt(qs_h, ks_h), 0.0).astype(BF16)
                qs_sub = jnp.concatenate([qs_h * cm_ref[slot, c] for c in range(N_SUB)], axis=1)
                st_cat = jnp.concatenate(entering, axis=1).astype(BF16)
                outs.append(_dot(s, vb_h) + _dot_nt(qs_sub, st_cat))
        return jnp.concatenate(outs, axis=1) if outs else None

    st_ref[...] = jnp.zeros_like(st_ref)

    def ctx_body(i, carry):
        scan_step(i, 0, 0, False, None, czf_ref, cv_ref)
        scan_step(n_ctx - 1 - i, 1, 0, False, None, czb_ref, cv_ref)
        return carry

    lax.fori_loop(0, n_ctx, ctx_body, 0)

    def lat_body(i, carry):
        store_rows(acc_ref.at[0], i, col_order, scan_step(i, 0, lat_slot, col_order, q_ref, zf_ref, v_ref))
        j = n_lat - 1 - i
        store_rows(acc_ref.at[1], j, col_order, scan_step(j, 1, lat_slot, col_order, q_ref, zb_ref, v_ref))
        return carry

    lax.fori_loop(0, n_lat, lat_body, 0)

    def readout_body(i, carry):
        blk = pl.ds(pl.multiple_of(i * SUPER, SUPER), SUPER)
        o = acc_ref[0, blk, :] + acc_ref[1, blk, :]
        g = g_ref[blk, :]
        res = []
        for h in range(heads):
            oh = o[:, hsl(h)]
            res.append(oh * lax.rsqrt(jnp.mean(oh * oh, axis=-1, keepdims=True) + EPS))
        o_ref[blk, :] = (jnp.concatenate(res, axis=1) * gain * (g * jax.nn.sigmoid(g))).astype(o_ref.dtype)
        return carry

    lax.fori_loop(0, n_lat, readout_body, 0)


def _hgrn(p_hg, p_ctx, lb_logits, gain, head0, n_heads, col_order, rows, name):
    b, seq, _ = p_hg.shape
    ctx_len = p_ctx.shape[1]
    assert seq % SUPER == 0 and ctx_len % SUPER == 0 and 2 * N_SUB == SUBLANES
    assert not col_order or (rows * SUBLANES == SUPER and GRID_W % SUBLANES == 0)
    hps = 2
    width = hps * HG_HEAD_DIM
    groups = n_heads // hps
    part_blocks = HG_WIDTH // width
    g0 = head0 // hps
    n_rows = lb_logits.shape[0]
    consts = _scan_constants((False, True) if col_order else (False,), rows)

    def slab(length, part):
        return pl.BlockSpec((None, length, width), lambda bi, gi: (bi, 0, part * part_blocks + g0 + gi))

    whole = lambda a: pl.BlockSpec(a.shape, lambda bi, gi: (0,) * a.ndim)
    kern = functools.partial(_hgrn_kernel, col_order=col_order, rows=rows)
    return pl.pallas_call(
        kern,
        grid=(b, groups),
        in_specs=[slab(seq, 0), slab(seq, 1), slab(seq, 2), slab(seq, 3), slab(seq, 4),
                  slab(ctx_len, 0), slab(ctx_len, 1), slab(ctx_len, 2),
                  pl.BlockSpec((n_rows, width), lambda bi, gi: (0, g0 + gi)),
                  pl.BlockSpec((1, width), lambda bi, gi: (0, g0 + gi))] + [whole(a) for a in consts],
        out_specs=pl.BlockSpec((None, seq, width), lambda bi, gi: (bi, 0, gi)),
        out_shape=jax.ShapeDtypeStruct((b, seq, n_heads * HG_HEAD_DIM), BF16),
        scratch_shapes=[pltpu.VMEM((2, seq, width), F32),
                        pltpu.VMEM((2 * hps, HG_HEAD_DIM, HG_HEAD_DIM), F32)],
        compiler_params=_cparams(("parallel", "parallel")),
        name=name,
    )(p_hg, p_hg, p_hg, p_hg, p_hg, p_ctx, p_ctx, p_ctx, lb_logits, gain, *consts)


def _dft_tables(seq):
    n = 2 * seq
    lo = GRID_W
    j = jnp.arange(n, dtype=jnp.int32)[:, None]
    f = jnp.where(j < seq, j, j - seq)
    ang = lambda m: (m % n).astype(F32) * (2.0 * math.pi / n)
    a = ang(f * (jnp.arange(seq // lo, dtype=jnp.int32) * lo)[None, :])[:, :, None]
    b = ang(f * jnp.arange(lo, dtype=jnp.int32)[None, :])[:, None, :]
    ca, sa, cb, sb = jnp.cos(a), jnp.sin(a), jnp.cos(b), jnp.sin(b)
    cos_ft = (ca * cb - sa * sb).reshape(n, seq)
    sin_ft = (sa * cb + ca * sb).reshape(n, seq)
    t = jnp.arange(seq, dtype=jnp.int32)[None, :]
    nyq = jnp.where(t % 2 == 0, 1.0, -1.0)
    tab = jnp.where(j < seq, cos_ft, jnp.where(j == seq, nyq, sin_ft)).astype(BF16)
    tab = lax.optimization_barrier(tab)
    return tab, tab.T


def _filter_embedding(seq):
    pos = jnp.arange(seq, dtype=F32)[:, None]
    t = pos / max(seq - 1, 1)
    bands = jnp.linspace(1e-4, N_BANDS - 1, N_BANDS, dtype=F32)[None, :]
    ang = bands * (2.0 * math.pi) * pos / seq
    return jnp.concatenate([t, jnp.cos(ang), -jnp.sin(ang)], axis=-1)


def _filter_deltas(width):
    return jnp.abs(jnp.linspace(math.log(FILTER_TARGET) / SLOW_DECAY_PCT,
                                math.log(FILTER_TARGET) / FAST_DECAY_PCT, width, dtype=F32))


def _kspec_kernel(z_ref, w1_ref, b1_ref, fr_ref, w2_ref, b2_ref, w3f_ref, w3b_ref, dl_ref, f_ref, k_ref):
    seq = z_ref.shape[0]
    ct = k_ref.shape[1]
    fr = fr_ref[...]
    h = jnp.sin(fr * (_dot3(z_ref[...], w1_ref[...]) + b1_ref[...]))
    h = jnp.sin(fr * (_dot3(h, w2_ref[...]) + b2_ref[...]))
    row = lax.broadcasted_iota(jnp.int32, (seq, ct), 0)
    t = row.astype(F32) / max(seq - 1, 1)
    window = jnp.exp(-t * dl_ref[...]) + FILTER_SHIFT
    hf = _dot3(h, w3f_ref[...]) * window
    hb = jnp.where(row == 0, 0.0, _dot3(h, w3b_ref[...]) * window)
    r = lax.rsqrt(jnp.sum(hf * hf, axis=0, keepdims=True) + jnp.sum(hb * hb, axis=0, keepdims=True))
    fmat = f_ref[...]
    kf = _dot(fmat, (hf * r).astype(BF16))
    kb = _dot(fmat, (hb * r).astype(BF16))
    jrow = lax.broadcasted_iota(jnp.int32, (2 * seq, ct), 0)
    k_ref[...] = kf + jnp.where(jrow > seq, -kb, kb)


def _kspec(z, w1, b1, freq, w2, b2, w3, deltas, fmat, ct):
    seq = z.shape[0]
    ch = deltas.shape[1]
    hid = w2.shape[0]
    nb = ch // ct
    full = lambda a: pl.BlockSpec(a.shape, lambda c: (0,) * a.ndim)
    return pl.pallas_call(
        _kspec_kernel,
        grid=(nb,),
        in_specs=[full(z), full(w1), full(b1), full(freq), full(w2), full(b2),
                  pl.BlockSpec((hid, ct), lambda c: (0, c)),
                  pl.BlockSpec((hid, ct), lambda c: (0, nb + c)),
                  pl.BlockSpec((1, ct), lambda c: (0, c)),
                  pl.BlockSpec(fmat.shape, lambda c: (0, 0), pipeline_mode=pl.Buffered(1))],
        out_specs=pl.BlockSpec((2 * seq, ct), lambda c: (0, c)),
        out_shape=jax.ShapeDtypeStruct((2 * seq, ch), F32),
        compiler_params=_cparams(("arbitrary",)),
        name="hyena_filter_spectrum",
    )(z, w1, b1, freq, w2, b2, w3, w3, deltas, fmat)


def _short_conv(t, w, b):
    seq = t.shape[0]
    row = lax.broadcasted_iota(jnp.int32, t.shape, 0)
    prev = jnp.where(row == 0, 0.0, pltpu.roll(t, 1, 0))
    nxt = jnp.where(row == seq - 1, 0.0, pltpu.roll(t, seq - 1, 0))
    return prev * w[0:1] + t * w[1:2] + nxt * w[2:3] + b


def _hy_fwd_kernel(v_ref, x1_ref, wv_ref, wx1_ref, bv_ref, bx1_ref, f_ref, k_ref, z_ref):
    seq = v_ref.shape[0]
    n = 2 * seq
    vc = _short_conv(v_ref[...].astype(F32), wv_ref[...], bv_ref[...])
    x1c = _short_conv(x1_ref[...].astype(F32), wx1_ref[...], bx1_ref[...])
    ub = (x1c * vc).astype(BF16)
    xa = _dot(f_ref[0:seq, :], ub)
    xb = _dot(f_ref[seq:n, :], ub)
    ka = k_ref[0:seq, :]
    kb = k_ref[seq:n, :]
    first = lax.broadcasted_iota(jnp.int32, xa.shape, 0) == 0
    w = jnp.where(first, 1.0 / n, 2.0 / n)
    bb = xb * kb
    z_ref[0:seq, :] = ((xa * ka - jnp.where(first, 0.0, bb)) * w).astype(z_ref.dtype)
    z_ref[seq:n, :] = (jnp.where(first, bb, xa * kb + xb * ka) * w).astype(z_ref.dtype)


def _hy_inv_kernel(z_ref, g_ref, v_ref, x1_ref, x0_ref, wv_ref, wx1_ref, wx0_ref, bv_ref, bx1_ref, bx0_ref,
                   bias_ref, o_ref):
    vc = _short_conv(v_ref[...].astype(F32), wv_ref[...], bv_ref[...])
    x1c = _short_conv(x1_ref[...].astype(F32), wx1_ref[...], bx1_ref[...])
    x0c = _short_conv(x0_ref[...].astype(F32), wx0_ref[...], bx0_ref[...])
    u = x1c * vc
    y = _dot(g_ref[...], z_ref[...])
    o_ref[...] = (x0c * (y + u * bias_ref[...])).astype(o_ref.dtype)


def _hyena(p_hy, col0, conv_w, conv_b, kspec, bias, fmat, gmat, ct):
    b, seq, _ = p_hy.shape
    ch = bias.shape[1]
    nb = ch // ct
    taps = conv_w.shape[0]
    assert col0 % ct == 0
    act = lambda part: pl.BlockSpec((None, seq, ct), lambda c, bi: (bi, 0, col0 // ct + part * nb + c))
    cw = lambda part: pl.BlockSpec((taps, ct), lambda c, bi: (0, part * nb + c))
    cb = lambda part: pl.BlockSpec((1, ct), lambda c, bi: (0, part * nb + c))
    const = lambda a: pl.BlockSpec(a.shape, lambda c, bi: (0, 0), pipeline_mode=pl.Buffered(1))
    zspec = pl.BlockSpec((None, 2 * seq, ct), lambda c, bi: (bi, 0, c))
    z = pl.pallas_call(
        _hy_fwd_kernel,
        grid=(nb, b),
        in_specs=[act(0), act(1), cw(0), cw(1), cb(0), cb(1), const(fmat),
                  pl.BlockSpec((2 * seq, ct), lambda c, bi: (0, c))],
        out_specs=zspec,
        out_shape=jax.ShapeDtypeStruct((b, 2 * seq, ch), BF16),
        compiler_params=_cparams(("parallel", "parallel")),
        name="hyena_forward_dft",
    )(p_hy, p_hy, conv_w, conv_w, conv_b, conv_b, fmat, kspec)
    return pl.pallas_call(
        _hy_inv_kernel,
        grid=(nb, b),
        in_specs=[zspec, const(gmat), act(0), act(1), act(2), cw(0), cw(1), cw(2), cb(0), cb(1), cb(2),
                  pl.BlockSpec((1, ct), lambda c, bi: (0, c))],
        out_specs=pl.BlockSpec((None, seq, ct), lambda c, bi: (bi, 0, c)),
        out_shape=jax.ShapeDtypeStruct((b, seq, ch), BF16),
        compiler_params=_cparams(("parallel", "parallel")),
        name="hyena_inverse_dft",
    )(z, gmat, p_hy, p_hy, p_hy, conv_w, conv_w, conv_w, conv_b, conv_b, conv_b, bias)


def _wout_kernel(ar_ref, ac_ref, y_ref, x_ref, gate_ref, w_ref, o_ref):
    mix = jnp.concatenate([ar_ref[...], ac_ref[...], y_ref[...]], axis=1)
    o_ref[...] = x_ref[...] + gate_ref[...] * _dot(mix, w_ref[...])


def _wout(a_row, a_col, y, x2, gate, w, rows_per_mod, tm):
    t, d = x2.shape
    m = gate.shape[0]
    per = rows_per_mod // tm
    wr, wc, wy = a_row.shape[1], a_col.shape[1], y.shape[1]
    return pl.pallas_call(
        _wout_kernel,
        grid=(t // tm,),
        in_specs=[pl.BlockSpec((tm, wr), lambda i: (i, 0)),
                  pl.BlockSpec((tm, wc), lambda i: (i, 0)),
                  pl.BlockSpec((tm, wy), lambda i: (i, 0)),
                  pl.BlockSpec((tm, d), lambda i: (i, 0)),
                  pl.BlockSpec((None, 1, d), lambda i: (i // per, 0, 0)),
                  pl.BlockSpec(w.shape, lambda i: (0, 0))],
        out_specs=pl.BlockSpec((tm, d), lambda i: (i, 0)),
        out_shape=jax.ShapeDtypeStruct((t, d), F32),
        compiler_params=_cparams(("parallel",)),
        name="out_projection_residual",
    )(a_row, a_col, y, x2, gate.reshape(m, 1, d), w)


def _mlp_kernel(x_ref, g_ref, sh_ref, sc_ref, gate_ref, w1_ref, w2_ref, fg_ref, o_ref, u_ref, acc_ref):
    j = pl.program_id(1)

    @pl.when(j == 0)
    def _():
        y = _rms(x_ref[...], g_ref[...])
        u_ref[...] = (y * (1.0 + sc_ref[...]) + sh_ref[...]).astype(BF16)
        acc_ref[...] = jnp.zeros_like(acc_ref)

    sub = u_ref.shape[0] // MLP_ROW_SPLIT
    for r in range(MLP_ROW_SPLIT):
        rs = slice(r * sub, (r + 1) * sub)
        h = jnp.maximum(_dot(u_ref[rs, :], w1_ref[...]), 0.0)
        acc_ref[rs, :] += _dot((h * h).astype(BF16), w2_ref[...])

    @pl.when(j == pl.num_programs(1) - 1)
    def _():
        o_ref[...] = _rms(x_ref[...] + gate_ref[...] * acc_ref[...], fg_ref[...])


def _mlp(x2, g, shift, scale, gate, w1, w2, fg, rows_per_mod, tm, tf):
    t, d = x2.shape
    dff = w1.shape[1]
    m = shift.shape[0]
    per = rows_per_mod // tm
    mod = pl.BlockSpec((None, 1, d), lambda i, j: (i // per, 0, 0))
    vec = pl.BlockSpec((1, d), lambda i, j: (0, 0))
    r3 = lambda a: a.reshape(m, 1, d)
    return pl.pallas_call(
        _mlp_kernel,
        grid=(t // tm, dff // tf),
        in_specs=[pl.BlockSpec((tm, d), lambda i, j: (i, 0)), vec, mod, mod, mod,
                  pl.BlockSpec((d, tf), lambda i, j: (0, j)),
                  pl.BlockSpec((tf, d), lambda i, j: (j, 0)),
                  vec],
        out_specs=pl.BlockSpec((tm, d), lambda i, j: (i, 0)),
        out_shape=jax.ShapeDtypeStruct((t, d), F32),
        scratch_shapes=[pltpu.VMEM((tm, d), BF16), pltpu.VMEM((tm, d), F32)],
        compiler_params=_cparams(("parallel", "arbitrary")),
        name="mlp_residual_final_norm",
    )(x2, g.reshape(1, d), r3(shift), r3(scale), r3(gate), w1, w2, fg.reshape(1, d))


def kernel(x, c, ctx, c_ctx, w_ada, b_ada, norm1_g, w_in, hgrn_lb_logits, hgrn_norm_g, hy_conv_w, hy_conv_b,
           flt_w1, flt_b1, flt_freq, flt_w2, flt_b2, flt_w3, hy_bias, w_out, norm2_g, w_mlp1, w_mlp2,
           final_norm_g):
    b, seq, d = x.shape
    ctx_len = ctx.shape[1]
    depth = w_ada.shape[0]
    assert depth == 1, "single-layer block"
    rows = seq // GRID_W
    hg_cols = 5 * HG_WIDTH
    hy_width = d - HG_WIDTH
    layer = 0

    pad = (-(b + 1)) % SUBLANES
    stacked = jnp.concatenate([c, c_ctx[None, :], jnp.zeros((pad, d), F32)], axis=0)
    mod = _ada(stacked, w_ada[layer], b_ada[layer])
    sh1, sc1, g1, sh2, sc2, g2 = [mod[:b, i * d:(i + 1) * d] for i in range(N_MOD)]
    csh1, csc1 = mod[b:b + 1, 0:d], mod[b:b + 1, d:2 * d]

    w_in_b = w_in[layer].astype(BF16)
    x2 = x.reshape(b * seq, d)
    ctx2 = ctx.reshape(b * ctx_len, d)
    p_lat = _norm_mod_matmul_pipelined(x2, norm1_g[layer], sh1, sc1, w_in_b, seq, 1024, 1024,
                                       "in_projection").reshape(b, seq, -1)
    p_ctx = _norm_mod_matmul(ctx2, norm1_g[layer], csh1, csc1, w_in_b[:, HG_WIDTH:4 * HG_WIDTH], b * ctx_len,
                             F32, min(1024, b * ctx_len), 1024, "in_projection_context")
    p_ctx = p_ctx.reshape(b, ctx_len, 3 * HG_WIDTH)

    lbl = hgrn_lb_logits.astype(F32).reshape(2 * (depth + 1), HG_WIDTH)
    gain = hgrn_norm_g[layer].reshape(1, HG_WIDTH)
    a_row = _hgrn(p_lat, p_ctx, lbl, gain, 0, HG_ROW_HEADS, False, rows, "hgrn_row_heads")
    a_col = _hgrn(p_lat, p_ctx, lbl, gain, HG_ROW_HEADS, HG_HEADS - HG_ROW_HEADS, True, rows, "hgrn_col_heads")

    fmat, gmat = _dft_tables(seq)
    emb = _filter_embedding(seq)
    kpad = (-emb.shape[1]) % LANES
    emb = jnp.pad(emb, ((0, 0), (0, kpad)))
    fw1 = jnp.pad(flt_w1[layer], ((0, kpad), (0, 0)))
    row = lambda a: a[layer].reshape(1, -1)
    kspec = _kspec(emb, fw1, row(flt_b1), row(flt_freq), flt_w2[layer], row(flt_b2), flt_w3[layer],
                   _filter_deltas(hy_width)[None, :], fmat, 256)
    y_lat = _hyena(p_lat, hg_cols, hy_conv_w[layer], hy_conv_b[layer].reshape(1, -1), kspec, row(hy_bias),
                   fmat, gmat, 256)

    x_mid = _wout(a_row.reshape(b * seq, -1), a_col.reshape(b * seq, -1), y_lat.reshape(b * seq, hy_width),
                  x2, g1, w_out[layer].astype(BF16), seq, 512)
    out = _mlp(x_mid, norm2_g[layer], sh2, sc2, g2, w_mlp1[layer].astype(BF16), w_mlp2[layer].astype(BF16),
               final_norm_g, seq, 512, 1024)
    return out.reshape(b, seq, d)
```

```python
import functools
import math

import jax
import jax.numpy as jnp
from jax import lax
from jax.experimental import pallas as pl
from jax.experimental.pallas import tpu as pltpu

GRID_W = 64
HG_WIDTH = 1024
HG_HEAD_DIM = 128
HG_HEADS = HG_WIDTH // HG_HEAD_DIM
HG_ROW_HEADS = HG_HEADS // 2
CHUNK = 64
N_BANDS = 16
FILTER_TARGET = 1e-2
FAST_DECAY_PCT = 0.3
SLOW_DECAY_PCT = 1.5
FILTER_SHIFT = 0.05
N_MOD = 6
EPS = 1e-6

F32 = jnp.float32
BF16 = jnp.bfloat16

V7X_VMEM_BYTES = 64 * 1024 * 1024
VMEM_LIMIT = 56 * 1024 * 1024
LANES = 128
SUBLANES = 8

SUPER = 256
N_SUB = SUPER // CHUNK

MLP_ROW_SPLIT = 2
HY_SPLIT = 2


def _cparams(sem):
    return pltpu.CompilerParams(dimension_semantics=sem, vmem_limit_bytes=VMEM_LIMIT)


def _split_bf16(a):
    hi = a.astype(BF16)
    lo = (a - hi.astype(F32)).astype(BF16)
    return hi, lo


def _dot(a, b):
    return jnp.dot(a, b, preferred_element_type=F32)


def _dot_nt(a, b):
    return lax.dot_general(a, b, (((1,), (1,)), ((), ())), preferred_element_type=F32)


def _dot_tn(a, b):
    return lax.dot_general(a, b, (((0,), (0,)), ((), ())), preferred_element_type=F32)


def _dot3(a, b):
    ah, al = _split_bf16(a)
    bh, bl = _split_bf16(b)
    return _dot(ah, bh) + _dot(ah, bl) + _dot(al, bh)


def _rms(x, g):
    return x * lax.rsqrt(jnp.mean(x * x, axis=-1, keepdims=True) + EPS) * g


def _ilog2(n):
    assert n > 0 and n & (n - 1) == 0, n
    return n.bit_length() - 1


def _ada_kernel(s_ref, w_ref, b_ref, o_ref):
    s = s_ref[...]
    s = s * jax.nn.sigmoid(s)
    o_ref[...] = _dot3(s, w_ref[...]) + b_ref[...]


def _ada(stacked, w, b):
    r, d = stacked.shape
    n = w.shape[1]
    tn = 512
    return pl.pallas_call(
        _ada_kernel,
        grid=(n // tn,),
        in_specs=[pl.BlockSpec((r, d), lambda j: (0, 0)),
                  pl.BlockSpec((d, tn), lambda j: (0, j)),
                  pl.BlockSpec((1, tn), lambda j: (0, j))],
        out_specs=pl.BlockSpec((r, tn), lambda j: (0, j)),
        out_shape=jax.ShapeDtypeStruct((r, n), F32),
        compiler_params=_cparams(("parallel",)),
        name="ada_modulation",
    )(stacked, w, b.reshape(1, n))


def _nmm_kernel(x_ref, g_ref, sh_ref, sc_ref, w_ref, o_ref, u_ref):
    @pl.when(pl.program_id(1) == 0)
    def _():
        y = _rms(x_ref[...], g_ref[...])
        u_ref[...] = (y * (1.0 + sc_ref[...]) + sh_ref[...]).astype(BF16)

    o_ref[...] = _dot(u_ref[...], w_ref[...]).astype(o_ref.dtype)


def _norm_mod_matmul(x2, g, shift, scale, w, rows_per_mod, out_dtype, tm, tn, name):
    t, d = x2.shape
    n = w.shape[1]
    m = shift.shape[0]
    per = rows_per_mod // tm
    mod_map = lambda i, j: (i // per, 0, 0)
    return pl.pallas_call(
        _nmm_kernel,
        grid=(t // tm, n // tn),
        in_specs=[pl.BlockSpec((tm, d), lambda i, j: (i, 0)),
                  pl.BlockSpec((1, d), lambda i, j: (0, 0)),
                  pl.BlockSpec((None, 1, d), mod_map),
                  pl.BlockSpec((None, 1, d), mod_map),
                  pl.BlockSpec((d, tn), lambda i, j: (0, j))],
        out_specs=pl.BlockSpec((tm, tn), lambda i, j: (i, j)),
        out_shape=jax.ShapeDtypeStruct((t, n), out_dtype),
        scratch_shapes=[pltpu.VMEM((tm, d), BF16)],
        compiler_params=_cparams(("parallel", "arbitrary")),
        name=name,
    )(x2, g.reshape(1, d), shift.reshape(m, 1, d), scale.reshape(m, 1, d), w)


def _nmm_pipelined_kernel(xs_ref, g_ref, sh_ref, sc_ref, w_ref, o_ref, u_ref):
    i = pl.program_id(0)
    j = pl.program_id(1)
    ts = xs_ref.shape[0]
    slot = i % 2

    @pl.when((i == 0) & (j == 0))
    def _():
        u_ref[1] = jnp.zeros(u_ref.shape[1:], u_ref.dtype)

    y = _rms(xs_ref[...], g_ref[...])
    u_ref[slot, pl.ds(pl.multiple_of(j * ts, ts), ts), :] = (y * (1.0 + sc_ref[...]) + sh_ref[...]).astype(BF16)
    o_ref[...] = _dot(u_ref[1 - slot], w_ref[...])


def _norm_mod_matmul_pipelined(x2, g, shift, scale, w, rows_per_mod, tm, tn, name):
    t, d = x2.shape
    n = w.shape[1]
    m = shift.shape[0]
    per = rows_per_mod // tm
    nt, nc = t // tm, n // tn
    ts = tm // nc
    assert ts * nc == tm and ts % SUBLANES == 0
    cur = lambda i: jnp.minimum(i, nt - 1)
    mod_map = lambda i, j: (cur(i) // per, 0, 0)
    return pl.pallas_call(
        _nmm_pipelined_kernel,
        grid=(nt + 1, nc),
        in_specs=[pl.BlockSpec((ts, d), lambda i, j: (cur(i) * nc + j, 0)),
                  pl.BlockSpec((1, d), lambda i, j: (0, 0)),
                  pl.BlockSpec((None, 1, d), mod_map),
                  pl.BlockSpec((None, 1, d), mod_map),
                  pl.BlockSpec((d, tn), lambda i, j: (0, j))],
        out_specs=pl.BlockSpec((tm, tn), lambda i, j: (jnp.maximum(i - 1, 0), jnp.where(i == 0, 0, j))),
        out_shape=jax.ShapeDtypeStruct((t, n), F32),
        scratch_shapes=[pltpu.VMEM((2, tm, d), BF16)],
        compiler_params=_cparams(("arbitrary", "arbitrary")),
        name=name,
    )(x2, g.reshape(1, d), shift.reshape(m, 1, d), scale.reshape(m, 1, d), w)


def _scan_pos(t, col_order, rows):
    if not col_order:
        return t
    cols = SUPER // rows
    return jnp.bitwise_and(t, cols - 1) * rows + jnp.right_shift(t, _ilog2(cols))


def _scan_constants(orders, rows):
    lc = _ilog2(CHUNK)
    half = CHUNK // 2
    dms, masks, rsels, cms = [], [], [], []
    for col_order in orders:
        pi = _scan_pos(lax.broadcasted_iota(jnp.int32, (SUPER, SUPER), 0), col_order, rows)
        pj = _scan_pos(lax.broadcasted_iota(jnp.int32, (SUPER, SUPER), 1), col_order, rows)
        same = jnp.right_shift(pi, lc) == jnp.right_shift(pj, lc)
        wj = jnp.bitwise_and(pj, CHUNK - 1)
        c8 = lax.broadcasted_iota(jnp.int32, (2 * N_SUB, SUPER), 0)
        p8 = _scan_pos(lax.broadcasted_iota(jnp.int32, (2 * N_SUB, SUPER), 1), col_order, rows)
        in_chunk = jnp.right_shift(p8, lc) == jnp.bitwise_and(c8, N_SUB - 1)
        w8 = jnp.bitwise_and(p8, CHUNK - 1)
        for fwd in (True, False):
            incl = same & ((pj <= pi) if fwd else (pj >= pi))
            upto_mid = same & ((wj < half) if fwd else (wj >= half))
            dms.append((incl.astype(F32) - upto_mid.astype(F32)).astype(BF16))
            masks.append(incl.astype(F32))
            sel = in_chunk & ((c8 >= N_SUB) | ((w8 < half) if fwd else (w8 >= half)))
            rsels.append(sel.astype(BF16))
        pr = jnp.right_shift(
            _scan_pos(lax.broadcasted_iota(jnp.int32, (SUPER, HG_HEAD_DIM), 0), col_order, rows), lc)
        cms.append(jnp.stack([(pr == c).astype(BF16) for c in range(N_SUB)]))
    return jnp.stack(dms), jnp.stack(masks), jnp.stack(rsels), jnp.stack(cms)


def _hgrn_kernel(q_ref, zf_ref, zb_ref, v_ref, g_ref, czf_ref, czb_ref, cv_ref, lbl_ref, gain_ref,
                 dm_ref, mask_ref, rsel_ref, cm_ref, o_ref, acc_ref, st_ref, *, col_order, rows):
    seq, width = q_ref.shape
    ctx_len = cv_ref.shape[0]
    heads = width // HG_HEAD_DIM
    n_lat = seq // SUPER
    n_ctx = ctx_len // SUPER
    n_slots = lbl_ref.shape[0] // 2
    cols = SUPER // rows
    d = HG_HEAD_DIM
    lat_slot = cm_ref.shape[0] - 1

    def lower_bound(direction):
        lg = lbl_ref[direction * n_slots:(direction + 1) * n_slots, :]
        e = jnp.exp(lg - jnp.max(lg, axis=0, keepdims=True))
        return e[0:1] / jnp.sum(e, axis=0, keepdims=True)

    lbs = (lower_bound(0), lower_bound(1))
    gain = gain_ref[...]

    def row_starts(sc, order):
        if not order:
            return [(pl.multiple_of(sc * SUPER, SUPER), SUPER)]
        return [(pl.multiple_of(r * GRID_W + sc * cols, cols), cols) for r in range(rows)]

    def load_rows(ref, sc, order):
        parts = [ref[pl.ds(s, n), :] for s, n in row_starts(sc, order)]
        return parts[0] if len(parts) == 1 else jnp.concatenate(parts, axis=0)

    def store_rows(ref, sc, order, val):
        off = 0
        for s, n in row_starts(sc, order):
            ref[pl.ds(s, n), :] = val[off:off + n]
            off += n

    def hsl(h):
        return slice(h * d, (h + 1) * d)

    def scan_step(sc, di, slot, order, q_r, z_r, v_r):
        fwd = di == 0
        k_idx = 2 * slot + di
        z = load_rows(z_r, sc, order)
        lb = lbs[di]
        f = lb + (1.0 - lb) * jax.nn.sigmoid(z)
        k = 1.0 - f
        lf_hi, lf_lo = _split_bf16(jnp.log(f))
        dm = dm_ref[k_idx]
        x1 = _dot(dm, lf_hi) + _dot(dm, lf_lo)
        rsel = rsel_ref[k_idx]
        ref_sums = _dot(rsel, lf_hi) + _dot(rsel, lf_lo)
        a_mid = ref_sums[0:N_SUB]
        a_end = ref_sums[N_SUB:2 * N_SUB]
        upd_scale = jnp.exp(a_end - a_mid)
        decay = jnp.exp(a_end)
        mid_scale = jnp.exp(a_mid)
        ks = (k * jnp.exp(-x1)).astype(BF16)
        vb = load_rows(v_r, sc, order).astype(BF16)
        if q_r is not None:
            qs = (load_rows(q_r, sc, order) * jnp.exp(x1)).astype(BF16)
            allowed = mask_ref[k_idx] != 0.0
        outs = []
        for h in range(heads):
            sl = hsl(h)
            ks_h = ks[:, sl]
            vb_h = vb[:, sl]
            ks_sub = jnp.concatenate([ks_h * cm_ref[slot, c] for c in range(N_SUB)], axis=1)
            upd = _dot_tn(vb_h, ks_sub)
            st = st_ref[di * heads + h]
            entering = [None] * N_SUB
            for c in (range(N_SUB) if fwd else reversed(range(N_SUB))):
                entering[c] = st * mid_scale[c:c + 1, sl]
                st = st * decay[c:c + 1, sl] + upd[:, c * d:(c + 1) * d] * upd_scale[c:c + 1, sl]
            st_ref[di * heads + h] = st
            if q_r is not None:
                qs_h = qs[:, sl]
                s = jnp.where(allowed, _dot_nt(qs_h, ks_h), 0.0).astype(BF16)
                qs_sub = jnp.concatenate([qs_h * cm_ref[slot, c] for c in range(N_SUB)], axis=1)
                st_cat = jnp.concatenate(entering, axis=1).astype(BF16)
                outs.append(_dot(s, vb_h) + _dot_nt(qs_sub, st_cat))
        return jnp.concatenate(outs, axis=1) if outs else None

    st_ref[...] = jnp.zeros_like(st_ref)

    def ctx_body(i, carry):
        scan_step(i, 0, 0, False, None, czf_ref, cv_ref)
        scan_step(n_ctx - 1 - i, 1, 0, False, None, czb_ref, cv_ref)
        return carry

    lax.fori_loop(0, n_ctx, ctx_body, 0)

    def lat_body(i, carry):
        store_rows(acc_ref.at[0], i, col_order, scan_step(i, 0, lat_slot, col_order, q_ref, zf_ref, v_ref))
        j = n_lat - 1 - i
        store_rows(acc_ref.at[1], j, col_order, scan_step(j, 1, lat_slot, col_order, q_ref, zb_ref, v_ref))
        return carry

    lax.fori_loop(0, n_lat, lat_body, 0)

    def readout_body(i, carry):
        blk = pl.ds(pl.multiple_of(i * SUPER, SUPER), SUPER)
        o = acc_ref[0, blk, :] + acc_ref[1, blk, :]
        g = g_ref[blk, :]
        res = []
        for h in range(heads):
            oh = o[:, hsl(h)]
            res.append(oh * lax.rsqrt(jnp.mean(oh * oh, axis=-1, keepdims=True) + EPS))
        o_ref[blk, :] = (jnp.concatenate(res, axis=1) * gain * (g * jax.nn.sigmoid(g))).astype(o_ref.dtype)
        return carry

    lax.fori_loop(0, n_lat, readout_body, 0)


def _hgrn(p_hg, p_ctx, lb_logits, gain, head0, n_heads, col_order, rows, name):
    b, seq, _ = p_hg.shape
    ctx_len = p_ctx.shape[1]
    assert seq % SUPER == 0 and ctx_len % SUPER == 0 and 2 * N_SUB == SUBLANES
    assert not col_order or (rows * SUBLANES == SUPER and GRID_W % SUBLANES == 0)
    hps = 2
    width = hps * HG_HEAD_DIM
    groups = n_heads // hps
    part_blocks = HG_WIDTH // width
    g0 = head0 // hps
    n_rows = lb_logits.shape[0]
    consts = _scan_constants((False, True) if col_order else (False,), rows)

    def slab(length, part):
        return pl.BlockSpec((None, length, width), lambda bi, gi: (bi, 0, part * part_blocks + g0 + gi))

    whole = lambda a: pl.BlockSpec(a.shape, lambda bi, gi: (0,) * a.ndim)
    kern = functools.partial(_hgrn_kernel, col_order=col_order, rows=rows)
    return pl.pallas_call(
        kern,
        grid=(b, groups),
        in_specs=[slab(seq, 0), slab(seq, 1), slab(seq, 2), slab(seq, 3), slab(seq, 4),
                  slab(ctx_len, 0), slab(ctx_len, 1), slab(ctx_len, 2),
                  pl.BlockSpec((n_rows, width), lambda bi, gi: (0, g0 + gi)),
                  pl.BlockSpec((1, width), lambda bi, gi: (0, g0 + gi))] + [whole(a) for a in consts],
        out_specs=pl.BlockSpec((None, seq, width), lambda bi, gi: (bi, 0, gi)),
        out_shape=jax.ShapeDtypeStruct((b, seq, n_heads * HG_HEAD_DIM), BF16),
        scratch_shapes=[pltpu.VMEM((2, seq, width), F32),
                        pltpu.VMEM((2 * hps, HG_HEAD_DIM, HG_HEAD_DIM), F32)],
        compiler_params=_cparams(("parallel", "parallel")),
        name=name,
    )(p_hg, p_hg, p_hg, p_hg, p_hg, p_ctx, p_ctx, p_ctx, lb_logits, gain, *consts)


def _dft_tables(seq):
    n = 2 * seq
    lo = GRID_W
    j = jnp.arange(n, dtype=jnp.int32)[:, None]
    f = jnp.where(j < seq, j, j - seq)
    ang = lambda m: (m % n).astype(F32) * (2.0 * math.pi / n)
    a = ang(f * (jnp.arange(seq // lo, dtype=jnp.int32) * lo)[None, :])[:, :, None]
    b = ang(f * jnp.arange(lo, dtype=jnp.int32)[None, :])[:, None, :]
    ca, sa, cb, sb = jnp.cos(a), jnp.sin(a), jnp.cos(b), jnp.sin(b)
    cos_ft = (ca * cb - sa * sb).reshape(n, seq)
    sin_ft = (sa * cb + ca * sb).reshape(n, seq)
    t = jnp.arange(seq, dtype=jnp.int32)[None, :]
    nyq = jnp.where(t % 2 == 0, 1.0, -1.0)
    tab = jnp.where(j < seq, cos_ft, jnp.where(j == seq, nyq, sin_ft)).astype(BF16)
    tab = lax.optimization_barrier(tab)
    return tab, tab.T


def _filter_embedding(seq):
    pos = jnp.arange(seq, dtype=F32)[:, None]
    t = pos / max(seq - 1, 1)
    bands = jnp.linspace(1e-4, N_BANDS - 1, N_BANDS, dtype=F32)[None, :]
    ang = bands * (2.0 * math.pi) * pos / seq
    return jnp.concatenate([t, jnp.cos(ang), -jnp.sin(ang)], axis=-1)


def _filter_deltas(width):
    return jnp.abs(jnp.linspace(math.log(FILTER_TARGET) / SLOW_DECAY_PCT,
                                math.log(FILTER_TARGET) / FAST_DECAY_PCT, width, dtype=F32))


def _kspec_kernel(z_ref, w1_ref, b1_ref, fr_ref, w2_ref, b2_ref, w3f_ref, w3b_ref, dl_ref, f_ref, k_ref):
    seq = z_ref.shape[0]
    ct = k_ref.shape[1]
    fr = fr_ref[...]
    h = jnp.sin(fr * (_dot3(z_ref[...], w1_ref[...]) + b1_ref[...]))
    h = jnp.sin(fr * (_dot3(h, w2_ref[...]) + b2_ref[...]))
    row = lax.broadcasted_iota(jnp.int32, (seq, ct), 0)
    t = row.astype(F32) / max(seq - 1, 1)
    window = jnp.exp(-t * dl_ref[...]) + FILTER_SHIFT
    hf = _dot3(h, w3f_ref[...]) * window
    hb = jnp.where(row == 0, 0.0, _dot3(h, w3b_ref[...]) * window)
    r = lax.rsqrt(jnp.sum(hf * hf, axis=0, keepdims=True) + jnp.sum(hb * hb, axis=0, keepdims=True))
    fmat = f_ref[...]
    kf = _dot(fmat, (hf * r).astype(BF16))
    kb = _dot(fmat, (hb * r).astype(BF16))
    jrow = lax.broadcasted_iota(jnp.int32, (2 * seq, ct), 0)
    k_ref[...] = kf + jnp.where(jrow > seq, -kb, kb)


def _kspec(z, w1, b1, freq, w2, b2, w3, deltas, fmat, ct):
    seq = z.shape[0]
    ch = deltas.shape[1]
    hid = w2.shape[0]
    nb = ch // ct
    full = lambda a: pl.BlockSpec(a.shape, lambda c: (0,) * a.ndim)
    return pl.pallas_call(
        _kspec_kernel,
        grid=(nb,),
        in_specs=[full(z), full(w1), full(b1), full(freq), full(w2), full(b2),
                  pl.BlockSpec((hid, ct), lambda c: (0, c)),
                  pl.BlockSpec((hid, ct), lambda c: (0, nb + c)),
                  pl.BlockSpec((1, ct), lambda c: (0, c)),
                  pl.BlockSpec(fmat.shape, lambda c: (0, 0), pipeline_mode=pl.Buffered(1))],
        out_specs=pl.BlockSpec((2 * seq, ct), lambda c: (0, c)),
        out_shape=jax.ShapeDtypeStruct((2 * seq, ch), F32),
        compiler_params=_cparams(("arbitrary",)),
        name="hyena_filter_spectrum",
    )(z, w1, b1, freq, w2, b2, w3, w3, deltas, fmat)


def _short_conv(t, w, b):
    seq = t.shape[0]
    row = lax.broadcasted_iota(jnp.int32, t.shape, 0)
    prev = jnp.where(row == 0, 0.0, pltpu.roll(t, 1, 0))
    nxt = jnp.where(row == seq - 1, 0.0, pltpu.roll(t, seq - 1, 0))
    return prev * w[0:1] + t * w[1:2] + nxt * w[2:3] + b


def _hy_fwd_kernel(v_ref, x1_ref, wv_ref, wx1_ref, bv_ref, bx1_ref, f_ref, k_ref, z_ref, u_ref):
    seq = v_ref.shape[0]
    n = 2 * seq
    vc = _short_conv(v_ref[...], wv_ref[...], bv_ref[...])
    x1c = _short_conv(x1_ref[...], wx1_ref[...], bx1_ref[...])
    ub = (x1c * vc).astype(BF16)
    u_ref[...] = ub
    fs = seq // HY_SPLIT
    for r in range(HY_SPLIT):
        xa = _dot(f_ref[r * fs:(r + 1) * fs, :], ub)
        xb = _dot(f_ref[seq + r * fs:seq + (r + 1) * fs, :], ub)
        ka = k_ref[r * fs:(r + 1) * fs, :]
        kb = k_ref[seq + r * fs:seq + (r + 1) * fs, :]
        bb = xb * kb
        if r == 0:
            first = lax.broadcasted_iota(jnp.int32, xa.shape, 0) == 0
            w = jnp.where(first, 1.0 / n, 2.0 / n)
            za = (xa * ka - jnp.where(first, 0.0, bb)) * w
            zb = jnp.where(first, bb, xa * kb + xb * ka) * w
        else:
            za = (xa * ka - bb) * (2.0 / n)
            zb = (xa * kb + xb * ka) * (2.0 / n)
        z_ref[r * fs:(r + 1) * fs, :] = za.astype(z_ref.dtype)
        z_ref[seq + r * fs:seq + (r + 1) * fs, :] = zb.astype(z_ref.dtype)


def _hy_inv_kernel(z_ref, g_ref, u_ref, x0_ref, wx0_ref, bx0_ref, bias_ref, o_ref):
    seq = x0_ref.shape[0]
    x0c = _short_conv(x0_ref[...], wx0_ref[...], bx0_ref[...])
    ts = seq // HY_SPLIT
    for r in range(HY_SPLIT):
        rs = slice(r * ts, (r + 1) * ts)
        y = _dot(g_ref[rs, :], z_ref[...])
        o_ref[rs, :] = (x0c[rs] * (y + u_ref[rs, :].astype(F32) * bias_ref[...])).astype(o_ref.dtype)


def _hyena(p_hy, col0, conv_w, conv_b, kspec, bias, fmat, gmat, ct):
    b, seq, _ = p_hy.shape
    ch = bias.shape[1]
    nb = ch // ct
    taps = conv_w.shape[0]
    assert col0 % ct == 0
    act = lambda part: pl.BlockSpec((None, seq, ct), lambda c, bi: (bi, 0, col0 // ct + part * nb + c))
    cw = lambda part: pl.BlockSpec((taps, ct), lambda c, bi: (0, part * nb + c))
    cb = lambda part: pl.BlockSpec((1, ct), lambda c, bi: (0, part * nb + c))
    const = lambda a: pl.BlockSpec(a.shape, lambda c, bi: (0, 0), pipeline_mode=pl.Buffered(1))
    zspec = pl.BlockSpec((None, 2 * seq, ct), lambda c, bi: (bi, 0, c))
    uspec = pl.BlockSpec((None, seq, ct), lambda c, bi: (bi, 0, c))
    z, u = pl.pallas_call(
        _hy_fwd_kernel,
        grid=(nb, b),
        in_specs=[act(0), act(1), cw(0), cw(1), cb(0), cb(1), const(fmat),
                  pl.BlockSpec((2 * seq, ct), lambda c, bi: (0, c))],
        out_specs=[zspec, uspec],
        out_shape=[jax.ShapeDtypeStruct((b, 2 * seq, ch), BF16), jax.ShapeDtypeStruct((b, seq, ch), BF16)],
        compiler_params=_cparams(("parallel", "parallel")),
        name="hyena_forward_dft",
    )(p_hy, p_hy, conv_w, conv_w, conv_b, conv_b, fmat, kspec)
    return pl.pallas_call(
        _hy_inv_kernel,
        grid=(nb, b),
        in_specs=[zspec, const(gmat), uspec, act(2), cw(2), cb(2), pl.BlockSpec((1, ct), lambda c, bi: (0, c))],
        out_specs=pl.BlockSpec((None, seq, ct), lambda c, bi: (bi, 0, c)),
        out_shape=jax.ShapeDtypeStruct((b, seq, ch), BF16),
        compiler_params=_cparams(("parallel", "parallel")),
        name="hyena_inverse_dft",
    )(z, gmat, u, p_hy, conv_w, conv_b, bias)


def _wout_kernel(ar_ref, ac_ref, y_ref, x_ref, gate_ref, w_ref, o_ref):
    mix = jnp.concatenate([ar_ref[...], ac_ref[...], y_ref[...]], axis=1)
    o_ref[...] = x_ref[...] + gate_ref[...] * _dot(mix, w_ref[...])


def _wout(a_row, a_col, y, x2, gate, w, rows_per_mod, tm):
    t, d = x2.shape
    m = gate.shape[0]
    per = rows_per_mod // tm
    wr, wc, wy = a_row.shape[1], a_col.shape[1], y.shape[1]
    return pl.pallas_call(
        _wout_kernel,
        grid=(t // tm,),
        in_specs=[pl.BlockSpec((tm, wr), lambda i: (i, 0)),
                  pl.BlockSpec((tm, wc), lambda i: (i, 0)),
                  pl.BlockSpec((tm, wy), lambda i: (i, 0)),
                  pl.BlockSpec((tm, d), lambda i: (i, 0)),
                  pl.BlockSpec((None, 1, d), lambda i: (i // per, 0, 0)),
                  pl.BlockSpec(w.shape, lambda i: (0, 0))],
        out_specs=pl.BlockSpec((tm, d), lambda i: (i, 0)),
        out_shape=jax.ShapeDtypeStruct((t, d), F32),
        compiler_params=_cparams(("parallel",)),
        name="out_projection_residual",
    )(a_row, a_col, y, x2, gate.reshape(m, 1, d), w)


def _mlp_kernel(x_ref, g_ref, sh_ref, sc_ref, gate_ref, w1_ref, w2_ref, fg_ref, o_ref, u_ref, acc_ref):
    j = pl.program_id(1)
    last = pl.num_programs(1) - 1
    sub = u_ref.shape[0] // MLP_ROW_SPLIT

    def step(first, final):
        for r in range(MLP_ROW_SPLIT):
            rs = slice(r * sub, (r + 1) * sub)
            if first:
                y = _rms(x_ref[rs, :], g_ref[...])
                u_ref[rs, :] = (y * (1.0 + sc_ref[...]) + sh_ref[...]).astype(BF16)
            h = jnp.maximum(_dot(u_ref[rs, :], w1_ref[...]), 0.0)
            acc = _dot((h * h).astype(BF16), w2_ref[...])
            if not first:
                acc = acc_ref[rs, :] + acc
            if final:
                o_ref[rs, :] = _rms(x_ref[rs, :] + gate_ref[...] * acc, fg_ref[...])
            else:
                acc_ref[rs, :] = acc

    pl.when(j == 0)(functools.partial(step, True, False))
    pl.when((j > 0) & (j < last))(functools.partial(step, False, False))
    pl.when(j == last)(functools.partial(step, False, True))


def _mlp(x2, g, shift, scale, gate, w1, w2, fg, rows_per_mod, tm, tf):
    t, d = x2.shape
    dff = w1.shape[1]
    m = shift.shape[0]
    per = rows_per_mod // tm
    mod = pl.BlockSpec((None, 1, d), lambda i, j: (i // per, 0, 0))
    vec = pl.BlockSpec((1, d), lambda i, j: (0, 0))
    r3 = lambda a: a.reshape(m, 1, d)
    return pl.pallas_call(
        _mlp_kernel,
        grid=(t // tm, dff // tf),
        in_specs=[pl.BlockSpec((tm, d), lambda i, j: (i, 0)), vec, mod, mod, mod,
                  pl.BlockSpec((d, tf), lambda i, j: (0, j)),
                  pl.BlockSpec((tf, d), lambda i, j: (j, 0)),
                  vec],
        out_specs=pl.BlockSpec((tm, d), lambda i, j: (i, 0)),
        out_shape=jax.ShapeDtypeStruct((t, d), F32),
        scratch_shapes=[pltpu.VMEM((tm, d), BF16), pltpu.VMEM((tm, d), F32)],
        compiler_params=_cparams(("parallel", "arbitrary")),
        name="mlp_residual_final_norm",
    )(x2, g.reshape(1, d), r3(shift), r3(scale), r3(gate), w1, w2, fg.reshape(1, d))


def kernel(x, c, ctx, c_ctx, w_ada, b_ada, norm1_g, w_in, hgrn_lb_logits, hgrn_norm_g, hy_conv_w, hy_conv_b,
           flt_w1, flt_b1, flt_freq, flt_w2, flt_b2, flt_w3, hy_bias, w_out, norm2_g, w_mlp1, w_mlp2,
           final_norm_g):
    b, seq, d = x.shape
    ctx_len = ctx.shape[1]
    depth = w_ada.shape[0]
    assert depth == 1, "single-layer block"
    rows = seq // GRID_W
    hg_cols = 5 * HG_WIDTH
    hy_width = d - HG_WIDTH
    layer = 0

    pad = (-(b + 1)) % SUBLANES
    stacked = jnp.concatenate([c, c_ctx[None, :], jnp.zeros((pad, d), F32)], axis=0)
    mod = _ada(stacked, w_ada[layer], b_ada[layer])
    sh1, sc1, g1, sh2, sc2, g2 = [mod[:b, i * d:(i + 1) * d] for i in range(N_MOD)]
    csh1, csc1 = mod[b:b + 1, 0:d], mod[b:b + 1, d:2 * d]

    w_in_b = w_in[layer].astype(BF16)
    x2 = x.reshape(b * seq, d)
    ctx2 = ctx.reshape(b * ctx_len, d)
    p_lat = _norm_mod_matmul_pipelined(x2, norm1_g[layer], sh1, sc1, w_in_b, seq, 1024, 2048,
                                       "in_projection").reshape(b, seq, -1)
    p_ctx = _norm_mod_matmul(ctx2, norm1_g[layer], csh1, csc1, w_in_b[:, HG_WIDTH:4 * HG_WIDTH], b * ctx_len,
                             F32, min(1024, b * ctx_len), 1024, "in_projection_context")
    p_ctx = p_ctx.reshape(b, ctx_len, 3 * HG_WIDTH)

    lbl = hgrn_lb_logits.astype(F32).reshape(2 * (depth + 1), HG_WIDTH)
    gain = hgrn_norm_g[layer].reshape(1, HG_WIDTH)
    a_row = _hgrn(p_lat, p_ctx, lbl, gain, 0, HG_ROW_HEADS, False, rows, "hgrn_row_heads")
    a_col = _hgrn(p_lat, p_ctx, lbl, gain, HG_ROW_HEADS, HG_HEADS - HG_ROW_HEADS, True, rows, "hgrn_col_heads")

    fmat, gmat = _dft_tables(seq)
    emb = _filter_embedding(seq)
    kpad = (-emb.shape[1]) % LANES
    emb = jnp.pad(emb, ((0, 0), (0, kpad)))
    fw1 = jnp.pad(flt_w1[layer], ((0, kpad), (0, 0)))
    row = lambda a: a[layer].reshape(1, -1)
    kspec = _kspec(emb, fw1, row(flt_b1), row(flt_freq), flt_w2[layer], row(flt_b2), flt_w3[layer],
                   _filter_deltas(hy_width)[None, :], fmat, 256)
    y_lat = _hyena(p_lat, hg_cols, hy_conv_w[layer], hy_conv_b[layer].reshape(1, -1), kspec, row(hy_bias),
                   fmat, gmat, 256)

    x_mid = _wout(a_row.reshape(b * seq, -1), a_col.reshape(b * seq, -1), y_lat.reshape(b * seq, hy_width),
                  x2, g1, w_out[layer].astype(BF16), seq, 512)
    out = _mlp(x_mid, norm2_g[layer], sh2, sc2, g2, w_mlp1[layer].astype(BF16), w_mlp2[layer].astype(BF16),
               final_norm_g, seq, 512, 1024)
    return out.reshape(b, seq, d)
```

```python
import functools
import math

import jax
import jax.numpy as jnp
from jax import lax
from jax.experimental import pallas as pl
from jax.experimental.pallas import tpu as pltpu

GRID_W = 64
HG_WIDTH = 1024
HG_HEAD_DIM = 128
HG_HEADS = HG_WIDTH // HG_HEAD_DIM
HG_ROW_HEADS = HG_HEADS // 2
CHUNK = 64
N_BANDS = 16
FILTER_TARGET = 1e-2
FAST_DECAY_PCT = 0.3
SLOW_DECAY_PCT = 1.5
FILTER_SHIFT = 0.05
N_MOD = 6
EPS = 1e-6

F32 = jnp.float32
BF16 = jnp.bfloat16

V7X_VMEM_BYTES = 64 * 1024 * 1024
VMEM_LIMIT = 56 * 1024 * 1024
LANES = 128
SUBLANES = 8

SUPER = 256
N_SUB = SUPER // CHUNK

MLP_ROW_SPLIT = 2
HY_SPLIT = 2


def _cparams(sem, flags=None):
    return pltpu.CompilerParams(dimension_semantics=sem, vmem_limit_bytes=VMEM_LIMIT, flags=flags)


def _split_bf16(a):
    hi = a.astype(BF16)
    lo = (a - hi.astype(F32)).astype(BF16)
    return hi, lo


def _dot(a, b):
    return jnp.dot(a, b, preferred_element_type=F32)


def _dot_nt(a, b):
    return lax.dot_general(a, b, (((1,), (1,)), ((), ())), preferred_element_type=F32)


def _dot_tn(a, b):
    return lax.dot_general(a, b, (((0,), (0,)), ((), ())), preferred_element_type=F32)


def _dot3(a, b):
    ah, al = _split_bf16(a)
    bh, bl = _split_bf16(b)
    return _dot(ah, bh) + _dot(ah, bl) + _dot(al, bh)


def _rms(x, g):
    return x * lax.rsqrt(jnp.mean(x * x, axis=-1, keepdims=True) + EPS) * g


def _ilog2(n):
    assert n > 0 and n & (n - 1) == 0, n
    return n.bit_length() - 1


def _ada_kernel(s_ref, w_ref, b_ref, o_ref):
    s = s_ref[...]
    s = s * jax.nn.sigmoid(s)
    o_ref[...] = _dot3(s, w_ref[...]) + b_ref[...]


def _ada(stacked, w, b):
    r, d = stacked.shape
    n = w.shape[1]
    tn = 512
    return pl.pallas_call(
        _ada_kernel,
        grid=(n // tn,),
        in_specs=[pl.BlockSpec((r, d), lambda j: (0, 0)),
                  pl.BlockSpec((d, tn), lambda j: (0, j)),
                  pl.BlockSpec((1, tn), lambda j: (0, j))],
        out_specs=pl.BlockSpec((r, tn), lambda j: (0, j)),
        out_shape=jax.ShapeDtypeStruct((r, n), F32),
        compiler_params=_cparams(("parallel",)),
        name="ada_modulation",
    )(stacked, w, b.reshape(1, n))


def _nmm_kernel(x_ref, g_ref, sh_ref, sc_ref, w_ref, o_ref, u_ref):
    @pl.when(pl.program_id(1) == 0)
    def _():
        y = _rms(x_ref[...], g_ref[...])
        u_ref[...] = (y * (1.0 + sc_ref[...]) + sh_ref[...]).astype(BF16)

    o_ref[...] = _dot(u_ref[...], w_ref[...]).astype(o_ref.dtype)


def _norm_mod_matmul(x2, g, shift, scale, w, rows_per_mod, out_dtype, tm, tn, name):
    t, d = x2.shape
    n = w.shape[1]
    m = shift.shape[0]
    per = rows_per_mod // tm
    mod_map = lambda i, j: (i // per, 0, 0)
    return pl.pallas_call(
        _nmm_kernel,
        grid=(t // tm, n // tn),
        in_specs=[pl.BlockSpec((tm, d), lambda i, j: (i, 0)),
                  pl.BlockSpec((1, d), lambda i, j: (0, 0)),
                  pl.BlockSpec((None, 1, d), mod_map),
                  pl.BlockSpec((None, 1, d), mod_map),
                  pl.BlockSpec((d, tn), lambda i, j: (0, j))],
        out_specs=pl.BlockSpec((tm, tn), lambda i, j: (i, j)),
        out_shape=jax.ShapeDtypeStruct((t, n), out_dtype),
        scratch_shapes=[pltpu.VMEM((tm, d), BF16)],
        compiler_params=_cparams(("parallel", "arbitrary")),
        name=name,
    )(x2, g.reshape(1, d), shift.reshape(m, 1, d), scale.reshape(m, 1, d), w)


def _nmm_pipelined_kernel(xs_ref, g_ref, sh_ref, sc_ref, w_ref, o_ref, ua_ref, ub_ref):
    i = pl.program_id(0)
    j = pl.program_id(1)
    ts = xs_ref.shape[0]

    @pl.when((i == 0) & (j == 0))
    def _():
        ub_ref[...] = jnp.zeros_like(ub_ref)

    def body(fill_ref, use_ref):
        y = _rms(xs_ref[...], g_ref[...])
        fill_ref[pl.ds(pl.multiple_of(j * ts, ts), ts), :] = (y * (1.0 + sc_ref[...]) + sh_ref[...]).astype(BF16)
        o_ref[...] = _dot(use_ref[...], w_ref[...])

    pl.when(i % 2 == 0)(functools.partial(body, ua_ref, ub_ref))
    pl.when(i % 2 == 1)(functools.partial(body, ub_ref, ua_ref))


def _norm_mod_matmul_pipelined(x2, g, shift, scale, w, rows_per_mod, tm, tn, name):
    t, d = x2.shape
    n = w.shape[1]
    m = shift.shape[0]
    per = rows_per_mod // tm
    nt, nc = t // tm, n // tn
    ts = tm // nc
    assert ts * nc == tm and ts % SUBLANES == 0
    cur = lambda i: jnp.minimum(i, nt - 1)
    mod_map = lambda i, j: (cur(i) // per, 0, 0)
    return pl.pallas_call(
        _nmm_pipelined_kernel,
        grid=(nt + 1, nc),
        in_specs=[pl.BlockSpec((ts, d), lambda i, j: (cur(i) * nc + j, 0)),
                  pl.BlockSpec((1, d), lambda i, j: (0, 0)),
                  pl.BlockSpec((None, 1, d), mod_map),
                  pl.BlockSpec((None, 1, d), mod_map),
                  pl.BlockSpec((d, tn), lambda i, j: (0, j))],
        out_specs=pl.BlockSpec((tm, tn), lambda i, j: (jnp.maximum(i - 1, 0), jnp.where(i == 0, 0, j))),
        out_shape=jax.ShapeDtypeStruct((t, n), F32),
        scratch_shapes=[pltpu.VMEM((tm, d), BF16), pltpu.VMEM((tm, d), BF16)],
        compiler_params=_cparams(("arbitrary", "arbitrary")),
        name=name,
    )(x2, g.reshape(1, d), shift.reshape(m, 1, d), scale.reshape(m, 1, d), w)


def _scan_pos(t, col_order, rows):
    if not col_order:
        return t
    cols = SUPER // rows
    return jnp.bitwise_and(t, cols - 1) * rows + jnp.right_shift(t, _ilog2(cols))


def _scan_constants(orders, rows):
    lc = _ilog2(CHUNK)
    half = CHUNK // 2
    dms, masks, rsels, cms = [], [], [], []
    for col_order in orders:
        pi = _scan_pos(lax.broadcasted_iota(jnp.int32, (SUPER, SUPER), 0), col_order, rows)
        pj = _scan_pos(lax.broadcasted_iota(jnp.int32, (SUPER, SUPER), 1), col_order, rows)
        same = jnp.right_shift(pi, lc) == jnp.right_shift(pj, lc)
        wj = jnp.bitwise_and(pj, CHUNK - 1)
        c8 = lax.broadcasted_iota(jnp.int32, (2 * N_SUB, SUPER), 0)
        p8 = _scan_pos(lax.broadcasted_iota(jnp.int32, (2 * N_SUB, SUPER), 1), col_order, rows)
        in_chunk = jnp.right_shift(p8, lc) == jnp.bitwise_and(c8, N_SUB - 1)
        w8 = jnp.bitwise_and(p8, CHUNK - 1)
        for fwd in (True, False):
            incl = same & ((pj <= pi) if fwd else (pj >= pi))
            upto_mid = same & ((wj < half) if fwd else (wj >= half))
            dms.append((incl.astype(F32) - upto_mid.astype(F32)).astype(BF16))
            masks.append(incl.astype(F32))
            sel = in_chunk & ((c8 >= N_SUB) | ((w8 < half) if fwd else (w8 >= half)))
            rsels.append(sel.astype(BF16))
        pr = jnp.right_shift(
            _scan_pos(lax.broadcasted_iota(jnp.int32, (SUPER, HG_HEAD_DIM), 0), col_order, rows), lc)
        cms.append(jnp.stack([(pr == c).astype(BF16) for c in range(N_SUB)]))
    return jnp.stack(dms), jnp.stack(masks), jnp.stack(rsels), jnp.stack(cms)


def _hgrn_kernel(q_ref, zf_ref, zb_ref, v_ref, g_ref, czf_ref, czb_ref, cv_ref, lbl_ref, gain_ref,
                 dm_ref, mask_ref, rsel_ref, cm_ref, o_ref, acc_ref, st_ref, *, col_order, rows):
    seq, width = q_ref.shape
    ctx_len = cv_ref.shape[0]
    heads = width // HG_HEAD_DIM
    n_lat = seq // SUPER
    n_ctx = ctx_len // SUPER
    n_slots = lbl_ref.shape[0] // 2
    cols = SUPER // rows
    d = HG_HEAD_DIM
    lat_slot = cm_ref.shape[0] - 1

    def lower_bound(direction):
        lg = lbl_ref[direction * n_slots:(direction + 1) * n_slots, :]
        e = jnp.exp(lg - jnp.max(lg, axis=0, keepdims=True))
        return e[0:1] / jnp.sum(e, axis=0, keepdims=True)

    lbs = (lower_bound(0), lower_bound(1))
    gain = gain_ref[...]

    def row_starts(sc, order):
        if not order:
            return [(pl.multiple_of(sc * SUPER, SUPER), SUPER)]
        return [(pl.multiple_of(r * GRID_W + sc * cols, cols), cols) for r in range(rows)]

    def load_rows(ref, sc, order):
        parts = [ref[pl.ds(s, n), :] for s, n in row_starts(sc, order)]
        return parts[0] if len(parts) == 1 else jnp.concatenate(parts, axis=0)

    def store_rows(ref, sc, order, val):
        off = 0
        for s, n in row_starts(sc, order):
            ref[pl.ds(s, n), :] = val[off:off + n]
            off += n

    def hsl(h):
        return slice(h * d, (h + 1) * d)

    def scan_step(sc, di, slot, order, q_r, z_r, v_r):
        fwd = di == 0
        k_idx = 2 * slot + di
        z = load_rows(z_r, sc, order)
        lb = lbs[di]
        f = lb + (1.0 - lb) * jax.nn.sigmoid(z)
        k = 1.0 - f
        lf_hi, lf_lo = _split_bf16(jnp.log(f))
        dm = dm_ref[k_idx]
        x1 = _dot(dm, lf_hi) + _dot(dm, lf_lo)
        rsel = rsel_ref[k_idx]
        ref_sums = _dot(rsel, lf_hi) + _dot(rsel, lf_lo)
        a_mid = ref_sums[0:N_SUB]
        a_end = ref_sums[N_SUB:2 * N_SUB]
        upd_scale = jnp.exp(a_end - a_mid)
        decay = jnp.exp(a_end)
        mid_scale = jnp.exp(a_mid)
        ks = (k * jnp.exp(-x1)).astype(BF16)
        vb = load_rows(v_r, sc, order).astype(BF16)
        if q_r is not None:
            qs = (load_rows(q_r, sc, order) * jnp.exp(x1)).astype(BF16)
            allowed = mask_ref[k_idx] != 0.0
        outs = []
        for h in range(heads):
            sl = hsl(h)
            ks_h = ks[:, sl]
            vb_h = vb[:, sl]
            ks_sub = jnp.concatenate([ks_h * cm_ref[slot, c] for c in range(N_SUB)], axis=1)
            upd = _dot_tn(vb_h, ks_sub)
            st = st_ref[di * heads + h]
            entering = [None] * N_SUB
            for c in (range(N_SUB) if fwd else reversed(range(N_SUB))):
                entering[c] = st * mid_scale[c:c + 1, sl]
                st = st * decay[c:c + 1, sl] + upd[:, c * d:(c + 1) * d] * upd_scale[c:c + 1, sl]
            st_ref[di * heads + h] = st
            if q_r is not None:
                qs_h = qs[:, sl]
                s = jnp.where(allowed, _dot_nt(qs_h, ks_h), 0.0).astype(BF16)
                qs_sub = jnp.concatenate([qs_h * cm_ref[slot, c] for c in range(N_SUB)], axis=1)
                st_cat = jnp.concatenate(entering, axis=1).astype(BF16)
                outs.append(_dot(s, vb_h) + _dot_nt(qs_sub, st_cat))
        return jnp.concatenate(outs, axis=1) if outs else None

    st_ref[...] = jnp.zeros_like(st_ref)

    def ctx_body(i, carry):
        scan_step(i, 0, 0, False, None, czf_ref, cv_ref)
        scan_step(n_ctx - 1 - i, 1, 0, False, None, czb_ref, cv_ref)
        return carry

    lax.fori_loop(0, n_ctx, ctx_body, 0)

    def lat_body(i, carry):
        store_rows(acc_ref.at[0], i, col_order, scan_step(i, 0, lat_slot, col_order, q_ref, zf_ref, v_ref))
        j = n_lat - 1 - i
        store_rows(acc_ref.at[1], j, col_order, scan_step(j, 1, lat_slot, col_order, q_ref, zb_ref, v_ref))
        return carry

    lax.fori_loop(0, n_lat, lat_body, 0, unroll=2)

    def readout_body(i, carry):
        blk = pl.ds(pl.multiple_of(i * SUPER, SUPER), SUPER)
        o = acc_ref[0, blk, :] + acc_ref[1, blk, :]
        g = g_ref[blk, :]
        res = []
        for h in range(heads):
            oh = o[:, hsl(h)]
            res.append(oh * lax.rsqrt(jnp.mean(oh * oh, axis=-1, keepdims=True) + EPS))
        o_ref[blk, :] = (jnp.concatenate(res, axis=1) * gain * (g * jax.nn.sigmoid(g))).astype(o_ref.dtype)
        return carry

    lax.fori_loop(0, n_lat, readout_body, 0)


def _hgrn(p_hg, p_ctx, lb_logits, gain, head0, n_heads, col_order, rows, name):
    b, seq, _ = p_hg.shape
    ctx_len = p_ctx.shape[1]
    assert seq % SUPER == 0 and ctx_len % SUPER == 0 and 2 * N_SUB == SUBLANES
    assert not col_order or (rows * SUBLANES == SUPER and GRID_W % SUBLANES == 0)
    hps = 2
    width = hps * HG_HEAD_DIM
    groups = n_heads // hps
    part_blocks = HG_WIDTH // width
    g0 = head0 // hps
    n_rows = lb_logits.shape[0]
    consts = _scan_constants((False, True) if col_order else (False,), rows)

    def slab(length, part):
        return pl.BlockSpec((None, length, width), lambda bi, gi: (bi, 0, part * part_blocks + g0 + gi))

    whole = lambda a: pl.BlockSpec(a.shape, lambda bi, gi: (0,) * a.ndim)
    kern = functools.partial(_hgrn_kernel, col_order=col_order, rows=rows)
    return pl.pallas_call(
        kern,
        grid=(b, groups),
        in_specs=[slab(seq, 0), slab(seq, 1), slab(seq, 2), slab(seq, 3), slab(seq, 4),
                  slab(ctx_len, 0), slab(ctx_len, 1), slab(ctx_len, 2),
                  pl.BlockSpec((n_rows, width), lambda bi, gi: (0, g0 + gi)),
                  pl.BlockSpec((1, width), lambda bi, gi: (0, g0 + gi))] + [whole(a) for a in consts],
        out_specs=pl.BlockSpec((None, seq, width), lambda bi, gi: (bi, 0, gi)),
        out_shape=jax.ShapeDtypeStruct((b, seq, n_heads * HG_HEAD_DIM), BF16),
        scratch_shapes=[pltpu.VMEM((2, seq, width), F32),
                        pltpu.VMEM((2 * hps, HG_HEAD_DIM, HG_HEAD_DIM), F32)],
        compiler_params=_cparams(("parallel", "parallel")),
        name=name,
    )(p_hg, p_hg, p_hg, p_hg, p_hg, p_ctx, p_ctx, p_ctx, lb_logits, gain, *consts)


def _dft_tables(seq):
    n = 2 * seq
    lo = GRID_W
    j = jnp.arange(n, dtype=jnp.int32)
    f = jnp.where(j < seq, j, j - seq)
    ang = lambda m: (m % n).astype(F32) * (2.0 * math.pi / n)
    a = ang(f[:, None] * (jnp.arange(seq // lo, dtype=jnp.int32) * lo)[None, :])
    b = ang(f[:, None] * jnp.arange(lo, dtype=jnp.int32)[None, :])
    ca, sa, cb, sb = jnp.cos(a), jnp.sin(a), jnp.cos(b), jnp.sin(b)
    t = jnp.arange(seq, dtype=jnp.int32)
    nyq = jnp.where(t % 2 == 0, 1.0, -1.0)

    def table(transposed):
        ex = (lambda m: m.T[:, None, :]) if transposed else (lambda m: m[:, :, None])
        ey = (lambda m: m.T[None, :, :]) if transposed else (lambda m: m[:, None, :])
        shape = (seq, n) if transposed else (n, seq)
        cos_ft = (ex(ca) * ey(cb) - ex(sa) * ey(sb)).reshape(shape)
        sin_ft = (ex(sa) * ey(cb) + ex(ca) * ey(sb)).reshape(shape)
        jj = j[None, :] if transposed else j[:, None]
        ny = nyq[:, None] if transposed else nyq[None, :]
        return jnp.where(jj < seq, cos_ft, jnp.where(jj == seq, ny, sin_ft)).astype(BF16)

    return table(False), table(True)


def _filter_embedding(seq):
    pos = jnp.arange(seq, dtype=F32)[:, None]
    t = pos / max(seq - 1, 1)
    bands = jnp.linspace(1e-4, N_BANDS - 1, N_BANDS, dtype=F32)[None, :]
    ang = bands * (2.0 * math.pi) * pos / seq
    return jnp.concatenate([t, jnp.cos(ang), -jnp.sin(ang)], axis=-1)


def _filter_deltas(width):
    return jnp.abs(jnp.linspace(math.log(FILTER_TARGET) / SLOW_DECAY_PCT,
                                math.log(FILTER_TARGET) / FAST_DECAY_PCT, width, dtype=F32))


def _kspec_kernel(z_ref, w1_ref, b1_ref, fr_ref, w2_ref, b2_ref, w3f_ref, w3b_ref, dl_ref, f_ref, k_ref, h_ref):
    seq = z_ref.shape[0]
    ct = k_ref.shape[1]

    @pl.when(pl.program_id(0) == 0)
    def _():
        fr = fr_ref[...]
        h1 = jnp.sin(fr * (_dot3(z_ref[...], w1_ref[...]) + b1_ref[...]))
        h_ref[...] = jnp.sin(fr * (_dot3(h1, w2_ref[...]) + b2_ref[...]))

    h = h_ref[...]
    row = lax.broadcasted_iota(jnp.int32, (seq, ct), 0)
    t = row.astype(F32) / max(seq - 1, 1)
    window = jnp.exp(-t * dl_ref[...]) + FILTER_SHIFT
    hf = _dot3(h, w3f_ref[...]) * window
    hb = jnp.where(row == 0, 0.0, _dot3(h, w3b_ref[...]) * window)
    r = lax.rsqrt(jnp.sum(hf * hf, axis=0, keepdims=True) + jnp.sum(hb * hb, axis=0, keepdims=True))
    plus = ((hf + hb) * r).astype(BF16)
    minus = ((hf - hb) * r).astype(BF16)
    k_ref[0:seq, :] = _dot(f_ref[0:seq, :], plus)
    k_ref[seq:2 * seq, :] = _dot(f_ref[seq:2 * seq, :], minus)
    nyq = _dot(f_ref[seq:seq + SUBLANES, :], plus)
    first = lax.broadcasted_iota(jnp.int32, (SUBLANES, ct), 0) == 0
    k_ref[seq:seq + SUBLANES, :] = jnp.where(first, nyq, k_ref[seq:seq + SUBLANES, :])


def _kspec(z, w1, b1, freq, w2, b2, w3, deltas, fmat, ct):
    seq = z.shape[0]
    ch = deltas.shape[1]
    hid = w2.shape[0]
    nb = ch // ct
    full = lambda a: pl.BlockSpec(a.shape, lambda c: (0,) * a.ndim)
    return pl.pallas_call(
        _kspec_kernel,
        grid=(nb,),
        in_specs=[full(z), full(w1), full(b1), full(freq), full(w2), full(b2),
                  pl.BlockSpec((hid, ct), lambda c: (0, c)),
                  pl.BlockSpec((hid, ct), lambda c: (0, nb + c)),
                  pl.BlockSpec((1, ct), lambda c: (0, c)),
                  pl.BlockSpec(fmat.shape, lambda c: (0, 0), pipeline_mode=pl.Buffered(1))],
        out_specs=pl.BlockSpec((2 * seq, ct), lambda c: (0, c)),
        out_shape=jax.ShapeDtypeStruct((2 * seq, ch), F32),
        scratch_shapes=[pltpu.VMEM((seq, hid), F32)],
        compiler_params=_cparams(("arbitrary",)),
        name="hyena_filter_spectrum",
    )(z, w1, b1, freq, w2, b2, w3, w3, deltas, fmat)


def _short_conv(t, w, b):
    seq = t.shape[0]
    row = lax.broadcasted_iota(jnp.int32, t.shape, 0)
    prev = jnp.where(row == 0, 0.0, pltpu.roll(t, 1, 0))
    nxt = jnp.where(row == seq - 1, 0.0, pltpu.roll(t, seq - 1, 0))
    return prev * w[0:1] + t * w[1:2] + nxt * w[2:3] + b


def _hy_fwd_kernel(v_ref, x1_ref, wv_ref, wx1_ref, bv_ref, bx1_ref, f_ref, k_ref, z_ref, u_ref):
    seq = v_ref.shape[0]
    n = 2 * seq
    vc = _short_conv(v_ref[...], wv_ref[...], bv_ref[...])
    x1c = _short_conv(x1_ref[...], wx1_ref[...], bx1_ref[...])
    ub = (x1c * vc).astype(BF16)
    u_ref[...] = ub
    fs = seq // HY_SPLIT
    for r in range(HY_SPLIT):
        xa = _dot(f_ref[r * fs:(r + 1) * fs, :], ub)
        xb = _dot(f_ref[seq + r * fs:seq + (r + 1) * fs, :], ub)
        ka = k_ref[r * fs:(r + 1) * fs, :]
        kb = k_ref[seq + r * fs:seq + (r + 1) * fs, :]
        bb = xb * kb
        if r == 0:
            first = lax.broadcasted_iota(jnp.int32, xa.shape, 0) == 0
            w = jnp.where(first, 1.0 / n, 2.0 / n)
            za = (xa * ka - jnp.where(first, 0.0, bb)) * w
            zb = jnp.where(first, bb, xa * kb + xb * ka) * w
        else:
            za = (xa * ka - bb) * (2.0 / n)
            zb = (xa * kb + xb * ka) * (2.0 / n)
        z_ref[r * fs:(r + 1) * fs, :] = za.astype(z_ref.dtype)
        z_ref[seq + r * fs:seq + (r + 1) * fs, :] = zb.astype(z_ref.dtype)


def _hy_inv_kernel(z_ref, g_ref, u_ref, x0_ref, wx0_ref, bx0_ref, bias_ref, o_ref):
    seq = x0_ref.shape[0]
    x0c = _short_conv(x0_ref[...], wx0_ref[...], bx0_ref[...])
    ts = seq // HY_SPLIT
    for r in range(HY_SPLIT):
        rs = slice(r * ts, (r + 1) * ts)
        y = _dot(g_ref[rs, :], z_ref[...])
        o_ref[rs, :] = (x0c[rs] * (y + u_ref[rs, :].astype(F32) * bias_ref[...])).astype(o_ref.dtype)


def _hyena(p_hy, col0, conv_w, conv_b, kspec, bias, fmat, gmat, ct):
    b, seq, _ = p_hy.shape
    ch = bias.shape[1]
    nb = ch // ct
    taps = conv_w.shape[0]
    assert col0 % ct == 0
    act = lambda part: pl.BlockSpec((None, seq, ct), lambda c, bi: (bi, 0, col0 // ct + part * nb + c))
    cw = lambda part: pl.BlockSpec((taps, ct), lambda c, bi: (0, part * nb + c))
    cb = lambda part: pl.BlockSpec((1, ct), lambda c, bi: (0, part * nb + c))
    const = lambda a: pl.BlockSpec(a.shape, lambda c, bi: (0, 0), pipeline_mode=pl.Buffered(1))
    zspec = pl.BlockSpec((None, 2 * seq, ct), lambda c, bi: (bi, 0, c))
    uspec = pl.BlockSpec((None, seq, ct), lambda c, bi: (bi, 0, c))
    z, u = pl.pallas_call(
        _hy_fwd_kernel,
        grid=(nb, b),
        in_specs=[act(0), act(1), cw(0), cw(1), cb(0), cb(1), const(fmat),
                  pl.BlockSpec((2 * seq, ct), lambda c, bi: (0, c))],
        out_specs=[zspec, uspec],
        out_shape=[jax.ShapeDtypeStruct((b, 2 * seq, ch), BF16), jax.ShapeDtypeStruct((b, seq, ch), BF16)],
        compiler_params=_cparams(("parallel", "parallel")),
        name="hyena_forward_dft",
    )(p_hy, p_hy, conv_w, conv_w, conv_b, conv_b, fmat, kspec)
    return pl.pallas_call(
        _hy_inv_kernel,
        grid=(nb, b),
        in_specs=[zspec, const(gmat), uspec, act(2), cw(2), cb(2), pl.BlockSpec((1, ct), lambda c, bi: (0, c))],
        out_specs=pl.BlockSpec((None, seq, ct), lambda c, bi: (bi, 0, c)),
        out_shape=jax.ShapeDtypeStruct((b, seq, ch), BF16),
        compiler_params=_cparams(("parallel", "parallel")),
        name="hyena_inverse_dft",
    )(z, gmat, u, p_hy, conv_w, conv_b, bias)


def _wout_kernel(ar_ref, ac_ref, y_ref, x_ref, gate_ref, w_ref, o_ref):
    mix = jnp.concatenate([ar_ref[...], ac_ref[...], y_ref[...]], axis=1)
    o_ref[...] = x_ref[...] + gate_ref[...] * _dot(mix, w_ref[...])


def _wout(a_row, a_col, y, x2, gate, w, rows_per_mod, tm):
    t, d = x2.shape
    m = gate.shape[0]
    per = rows_per_mod // tm
    wr, wc, wy = a_row.shape[1], a_col.shape[1], y.shape[1]
    return pl.pallas_call(
        _wout_kernel,
        grid=(t // tm,),
        in_specs=[pl.BlockSpec((tm, wr), lambda i: (i, 0)),
                  pl.BlockSpec((tm, wc), lambda i: (i, 0)),
                  pl.BlockSpec((tm, wy), lambda i: (i, 0)),
                  pl.BlockSpec((tm, d), lambda i: (i, 0)),
                  pl.BlockSpec((None, 1, d), lambda i: (i // per, 0, 0)),
                  pl.BlockSpec(w.shape, lambda i: (0, 0))],
        out_specs=pl.BlockSpec((tm, d), lambda i: (i, 0)),
        out_shape=jax.ShapeDtypeStruct((t, d), F32),
        compiler_params=_cparams(("parallel",)),
        name="out_projection_residual",
    )(a_row, a_col, y, x2, gate.reshape(m, 1, d), w)


def _mlp_kernel(x_ref, g_ref, sh_ref, sc_ref, gate_ref, w1_ref, w2_ref, fg_ref, o_ref, u_ref, acc_ref):
    j = pl.program_id(1)
    last = pl.num_programs(1) - 1
    sub = u_ref.shape[0] // MLP_ROW_SPLIT

    def step(first, final):
        for r in range(MLP_ROW_SPLIT):
            rs = slice(r * sub, (r + 1) * sub)
            if first:
                y = _rms(x_ref[rs, :], g_ref[...])
                u_ref[rs, :] = (y * (1.0 + sc_ref[...]) + sh_ref[...]).astype(BF16)
            h = jnp.maximum(_dot(u_ref[rs, :], w1_ref[...]), 0.0)
            acc = _dot((h * h).astype(BF16), w2_ref[...])
            if not first:
                acc = acc_ref[rs, :] + acc
            if final:
                o_ref[rs, :] = _rms(x_ref[rs, :] + gate_ref[...] * acc, fg_ref[...])
            else:
                acc_ref[rs, :] = acc

    pl.when(j == 0)(functools.partial(step, True, False))
    pl.when((j > 0) & (j < last))(functools.partial(step, False, False))
    pl.when(j == last)(functools.partial(step, False, True))


def _mlp(x2, g, shift, scale, gate, w1, w2, fg, rows_per_mod, tm, tf):
    t, d = x2.shape
    dff = w1.shape[1]
    m = shift.shape[0]
    per = rows_per_mod // tm
    mod = pl.BlockSpec((None, 1, d), lambda i, j: (i // per, 0, 0))
    vec = pl.BlockSpec((1, d), lambda i, j: (0, 0))
    r3 = lambda a: a.reshape(m, 1, d)
    return pl.pallas_call(
        _mlp_kernel,
        grid=(t // tm, dff // tf),
        in_specs=[pl.BlockSpec((tm, d), lambda i, j: (i, 0)), vec, mod, mod, mod,
                  pl.BlockSpec((d, tf), lambda i, j: (0, j)),
                  pl.BlockSpec((tf, d), lambda i, j: (j, 0)),
                  vec],
        out_specs=pl.BlockSpec((tm, d), lambda i, j: (i, 0)),
        out_shape=jax.ShapeDtypeStruct((t, d), F32),
        scratch_shapes=[pltpu.VMEM((tm, d), BF16), pltpu.VMEM((tm, d), F32)],
        compiler_params=_cparams(("parallel", "arbitrary")),
        name="mlp_residual_final_norm",
    )(x2, g.reshape(1, d), r3(shift), r3(scale), r3(gate), w1, w2, fg.reshape(1, d))


def kernel(x, c, ctx, c_ctx, w_ada, b_ada, norm1_g, w_in, hgrn_lb_logits, hgrn_norm_g, hy_conv_w, hy_conv_b,
           flt_w1, flt_b1, flt_freq, flt_w2, flt_b2, flt_w3, hy_bias, w_out, norm2_g, w_mlp1, w_mlp2,
           final_norm_g):
    b, seq, d = x.shape
    ctx_len = ctx.shape[1]
    depth = w_ada.shape[0]
    assert depth == 1, "single-layer block"
    rows = seq // GRID_W
    hg_cols = 5 * HG_WIDTH
    hy_width = d - HG_WIDTH
    layer = 0

    pad = (-(b + 1)) % SUBLANES
    stacked = jnp.concatenate([c, c_ctx[None, :], jnp.zeros((pad, d), F32)], axis=0)
    mod = _ada(stacked, w_ada[layer], b_ada[layer])
    sh1, sc1, g1, sh2, sc2, g2 = [mod[:b, i * d:(i + 1) * d] for i in range(N_MOD)]
    csh1, csc1 = mod[b:b + 1, 0:d], mod[b:b + 1, d:2 * d]

    w_in_b = w_in[layer].astype(BF16)
    x2 = x.reshape(b * seq, d)
    ctx2 = ctx.reshape(b * ctx_len, d)
    p_lat = _norm_mod_matmul_pipelined(x2, norm1_g[layer], sh1, sc1, w_in_b, seq, 1024, 2048,
                                       "in_projection").reshape(b, seq, -1)
    p_ctx = _norm_mod_matmul(ctx2, norm1_g[layer], csh1, csc1, w_in_b[:, HG_WIDTH:4 * HG_WIDTH], b * ctx_len,
                             F32, min(1024, b * ctx_len), 1024, "in_projection_context")
    p_ctx = p_ctx.reshape(b, ctx_len, 3 * HG_WIDTH)

    lbl = hgrn_lb_logits.astype(F32).reshape(2 * (depth + 1), HG_WIDTH)
    gain = hgrn_norm_g[layer].reshape(1, HG_WIDTH)
    a_row = _hgrn(p_lat, p_ctx, lbl, gain, 0, HG_ROW_HEADS, False, rows, "hgrn_row_heads")
    a_col = _hgrn(p_lat, p_ctx, lbl, gain, HG_ROW_HEADS, HG_HEADS - HG_ROW_HEADS, True, rows, "hgrn_col_heads")

    fmat, gmat = _dft_tables(seq)
    emb = _filter_embedding(seq)
    kpad = (-emb.shape[1]) % LANES
    emb = jnp.pad(emb, ((0, 0), (0, kpad)))
    fw1 = jnp.pad(flt_w1[layer], ((0, kpad), (0, 0)))
    row = lambda a: a[layer].reshape(1, -1)
    kspec = _kspec(emb, fw1, row(flt_b1), row(flt_freq), flt_w2[layer], row(flt_b2), flt_w3[layer],
                   _filter_deltas(hy_width)[None, :], fmat, 256)
    y_lat = _hyena(p_lat, hg_cols, hy_conv_w[layer], hy_conv_b[layer].reshape(1, -1), kspec, row(hy_bias),
                   fmat, gmat, 256)

    x_mid = _wout(a_row.reshape(b * seq, -1), a_col.reshape(b * seq, -1), y_lat.reshape(b * seq, hy_width),
                  x2, g1, w_out[layer].astype(BF16), seq, 512)
    out = _mlp(x_mid, norm2_g[layer], sh2, sc2, g2, w_mlp1[layer].astype(BF16), w_mlp2[layer].astype(BF16),
               final_norm_g, seq, 512, 1024)
    return out.reshape(b, seq, d)
```

```python
import functools
import math

import jax
import jax.numpy as jnp
from jax import lax
from jax.experimental import pallas as pl
from jax.experimental.pallas import tpu as pltpu

GRID_W = 64
HG_WIDTH = 1024
HG_HEAD_DIM = 128
HG_HEADS = HG_WIDTH // HG_HEAD_DIM
HG_ROW_HEADS = HG_HEADS // 2
CHUNK = 64
N_BANDS = 16
FILTER_TARGET = 1e-2
FAST_DECAY_PCT = 0.3
SLOW_DECAY_PCT = 1.5
FILTER_SHIFT = 0.05
N_MOD = 6
EPS = 1e-6

F32 = jnp.float32
BF16 = jnp.bfloat16

V7X_VMEM_BYTES = 64 * 1024 * 1024
VMEM_LIMIT = 56 * 1024 * 1024
LANES = 128
SUBLANES = 8

SUPER = 256
N_SUB = SUPER // CHUNK

MLP_ROW_SPLIT = 2
HY_SPLIT = 2


def _cparams(sem, flags=None):
    return pltpu.CompilerParams(dimension_semantics=sem, vmem_limit_bytes=VMEM_LIMIT, flags=flags)


def _split_bf16(a):
    hi = a.astype(BF16)
    lo = (a - hi.astype(F32)).astype(BF16)
    return hi, lo


def _dot(a, b):
    return jnp.dot(a, b, preferred_element_type=F32)


def _dot_nt(a, b):
    return lax.dot_general(a, b, (((1,), (1,)), ((), ())), preferred_element_type=F32)


def _dot_tn(a, b):
    return lax.dot_general(a, b, (((0,), (0,)), ((), ())), preferred_element_type=F32)


def _dot3(a, b):
    ah, al = _split_bf16(a)
    bh, bl = _split_bf16(b)
    return _dot(ah, bh) + _dot(ah, bl) + _dot(al, bh)


def _rms(x, g):
    return x * lax.rsqrt(jnp.mean(x * x, axis=-1, keepdims=True) + EPS) * g


def _ilog2(n):
    assert n > 0 and n & (n - 1) == 0, n
    return n.bit_length() - 1


def _ada_kernel(s_ref, w_ref, b_ref, o_ref):
    s = s_ref[...]
    s = s * jax.nn.sigmoid(s)
    o_ref[...] = _dot3(s, w_ref[...]) + b_ref[...]


def _ada(stacked, w, b):
    r, d = stacked.shape
    n = w.shape[1]
    tn = 512
    return pl.pallas_call(
        _ada_kernel,
        grid=(n // tn,),
        in_specs=[pl.BlockSpec((r, d), lambda j: (0, 0)),
                  pl.BlockSpec((d, tn), lambda j: (0, j)),
                  pl.BlockSpec((1, tn), lambda j: (0, j))],
        out_specs=pl.BlockSpec((r, tn), lambda j: (0, j)),
        out_shape=jax.ShapeDtypeStruct((r, n), F32),
        compiler_params=_cparams(("parallel",)),
        name="ada_modulation",
    )(stacked, w, b.reshape(1, n))


def _nmm_kernel(x_ref, g_ref, sh_ref, sc_ref, w_ref, o_ref, u_ref):
    @pl.when(pl.program_id(1) == 0)
    def _():
        y = _rms(x_ref[...], g_ref[...])
        u_ref[...] = (y * (1.0 + sc_ref[...]) + sh_ref[...]).astype(BF16)

    o_ref[...] = _dot(u_ref[...], w_ref[...]).astype(o_ref.dtype)


def _norm_mod_matmul(x2, g, shift, scale, w, rows_per_mod, out_dtype, tm, tn, name):
    t, d = x2.shape
    n = w.shape[1]
    m = shift.shape[0]
    per = rows_per_mod // tm
    mod_map = lambda i, j: (i // per, 0, 0)
    return pl.pallas_call(
        _nmm_kernel,
        grid=(t // tm, n // tn),
        in_specs=[pl.BlockSpec((tm, d), lambda i, j: (i, 0)),
                  pl.BlockSpec((1, d), lambda i, j: (0, 0)),
                  pl.BlockSpec((None, 1, d), mod_map),
                  pl.BlockSpec((None, 1, d), mod_map),
                  pl.BlockSpec((d, tn), lambda i, j: (0, j))],
        out_specs=pl.BlockSpec((tm, tn), lambda i, j: (i, j)),
        out_shape=jax.ShapeDtypeStruct((t, n), out_dtype),
        scratch_shapes=[pltpu.VMEM((tm, d), BF16)],
        compiler_params=_cparams(("parallel", "arbitrary")),
        name=name,
    )(x2, g.reshape(1, d), shift.reshape(m, 1, d), scale.reshape(m, 1, d), w)


def _nmm_pipelined_kernel(xs_ref, g_ref, sh_ref, sc_ref, w_ref, o_ref, ua_ref, ub_ref):
    i = pl.program_id(0)
    j = pl.program_id(1)
    ts = xs_ref.shape[0]

    @pl.when((i == 0) & (j == 0))
    def _():
        ub_ref[...] = jnp.zeros_like(ub_ref)

    def body(fill_ref, use_ref):
        y = _rms(xs_ref[...], g_ref[...])
        fill_ref[pl.ds(pl.multiple_of(j * ts, ts), ts), :] = (y * (1.0 + sc_ref[...]) + sh_ref[...]).astype(BF16)
        o_ref[...] = _dot(use_ref[...], w_ref[...])

    pl.when(i % 2 == 0)(functools.partial(body, ua_ref, ub_ref))
    pl.when(i % 2 == 1)(functools.partial(body, ub_ref, ua_ref))


def _norm_mod_matmul_pipelined(x2, g, shift, scale, w, rows_per_mod, tm, tn, name):
    t, d = x2.shape
    n = w.shape[1]
    m = shift.shape[0]
    per = rows_per_mod // tm
    nt, nc = t // tm, n // tn
    ts = tm // nc
    assert ts * nc == tm and ts % SUBLANES == 0
    cur = lambda i: jnp.minimum(i, nt - 1)
    mod_map = lambda i, j: (cur(i) // per, 0, 0)
    return pl.pallas_call(
        _nmm_pipelined_kernel,
        grid=(nt + 1, nc),
        in_specs=[pl.BlockSpec((ts, d), lambda i, j: (cur(i) * nc + j, 0)),
                  pl.BlockSpec((1, d), lambda i, j: (0, 0)),
                  pl.BlockSpec((None, 1, d), mod_map),
                  pl.BlockSpec((None, 1, d), mod_map),
                  pl.BlockSpec((d, tn), lambda i, j: (0, j))],
        out_specs=pl.BlockSpec((tm, tn), lambda i, j: (jnp.maximum(i - 1, 0), jnp.where(i == 0, 0, j))),
        out_shape=jax.ShapeDtypeStruct((t, n), F32),
        scratch_shapes=[pltpu.VMEM((tm, d), BF16), pltpu.VMEM((tm, d), BF16)],
        compiler_params=_cparams(("arbitrary", "arbitrary")),
        name=name,
    )(x2, g.reshape(1, d), shift.reshape(m, 1, d), scale.reshape(m, 1, d), w)


def _scan_pos(t, col_order, rows):
    if not col_order:
        return t
    cols = SUPER // rows
    return jnp.bitwise_and(t, cols - 1) * rows + jnp.right_shift(t, _ilog2(cols))


def _scan_constants(orders, rows):
    lc = _ilog2(CHUNK)
    half = CHUNK // 2
    dms, masks, rsels, cms = [], [], [], []
    for col_order in orders:
        pi = _scan_pos(lax.broadcasted_iota(jnp.int32, (SUPER, SUPER), 0), col_order, rows)
        pj = _scan_pos(lax.broadcasted_iota(jnp.int32, (SUPER, SUPER), 1), col_order, rows)
        same = jnp.right_shift(pi, lc) == jnp.right_shift(pj, lc)
        wj = jnp.bitwise_and(pj, CHUNK - 1)
        c8 = lax.broadcasted_iota(jnp.int32, (2 * N_SUB, SUPER), 0)
        p8 = _scan_pos(lax.broadcasted_iota(jnp.int32, (2 * N_SUB, SUPER), 1), col_order, rows)
        in_chunk = jnp.right_shift(p8, lc) == jnp.bitwise_and(c8, N_SUB - 1)
        w8 = jnp.bitwise_and(p8, CHUNK - 1)
        for fwd in (True, False):
            incl = same & ((pj <= pi) if fwd else (pj >= pi))
            upto_mid = same & ((wj < half) if fwd else (wj >= half))
            dms.append((incl.astype(F32) - upto_mid.astype(F32)).astype(BF16))
            masks.append(incl.astype(F32))
            sel = in_chunk & ((c8 >= N_SUB) | ((w8 < half) if fwd else (w8 >= half)))
            rsels.append(sel.astype(BF16))
        pr = jnp.right_shift(
            _scan_pos(lax.broadcasted_iota(jnp.int32, (SUPER, HG_HEAD_DIM), 0), col_order, rows), lc)
        cms.append(jnp.stack([(pr == c).astype(BF16) for c in range(N_SUB)]))
    return jnp.stack(dms), jnp.stack(masks), jnp.stack(rsels), jnp.stack(cms)


def _hgrn_kernel(q_ref, zf_ref, zb_ref, v_ref, g_ref, czf_ref, czb_ref, cv_ref, lbl_ref, gain_ref,
                 dm_ref, mask_ref, rsel_ref, cm_ref, o_ref, acc_ref, st_ref, *, col_order, rows):
    seq, width = q_ref.shape
    ctx_len = cv_ref.shape[0]
    heads = width // HG_HEAD_DIM
    n_lat = seq // SUPER
    n_ctx = ctx_len // SUPER
    n_slots = lbl_ref.shape[0] // 2
    cols = SUPER // rows
    d = HG_HEAD_DIM
    lat_slot = cm_ref.shape[0] - 1

    def lower_bound(direction):
        lg = lbl_ref[direction * n_slots:(direction + 1) * n_slots, :]
        e = jnp.exp(lg - jnp.max(lg, axis=0, keepdims=True))
        return e[0:1] / jnp.sum(e, axis=0, keepdims=True)

    lbs = (lower_bound(0), lower_bound(1))
    gain = gain_ref[...]

    def row_starts(sc, order):
        if not order:
            return [(pl.multiple_of(sc * SUPER, SUPER), SUPER)]
        return [(pl.multiple_of(r * GRID_W + sc * cols, cols), cols) for r in range(rows)]

    def load_rows(ref, sc, order):
        parts = [ref[pl.ds(s, n), :] for s, n in row_starts(sc, order)]
        return parts[0] if len(parts) == 1 else jnp.concatenate(parts, axis=0)

    def store_rows(ref, sc, order, val):
        off = 0
        for s, n in row_starts(sc, order):
            ref[pl.ds(s, n), :] = val[off:off + n]
            off += n

    def hsl(h):
        return slice(h * d, (h + 1) * d)

    def scan_step(sc, di, slot, order, q_r, z_r, v_r):
        fwd = di == 0
        k_idx = 2 * slot + di
        z = load_rows(z_r, sc, order)
        lb = lbs[di]
        f = lb + (1.0 - lb) * jax.nn.sigmoid(z)
        k = 1.0 - f
        lf_hi, lf_lo = _split_bf16(jnp.log(f))
        dm = dm_ref[k_idx]
        x1 = _dot(dm, lf_hi) + _dot(dm, lf_lo)
        rsel = rsel_ref[k_idx]
        ref_sums = _dot(rsel, lf_hi) + _dot(rsel, lf_lo)
        a_mid = ref_sums[0:N_SUB]
        a_end = ref_sums[N_SUB:2 * N_SUB]
        upd_scale = jnp.exp(a_end - a_mid)
        decay = jnp.exp(a_end)
        mid_scale = jnp.exp(a_mid)
        ks = (k * jnp.exp(-x1)).astype(BF16)
        vb = load_rows(v_r, sc, order).astype(BF16)
        if q_r is not None:
            qs = (load_rows(q_r, sc, order) * jnp.exp(x1)).astype(BF16)
            allowed = mask_ref[k_idx] != 0.0
        outs = []
        for h in range(heads):
            sl = hsl(h)
            ks_h = ks[:, sl]
            vb_h = vb[:, sl]
            ks_sub = jnp.concatenate([ks_h * cm_ref[slot, c] for c in range(N_SUB)], axis=1)
            upd = _dot_tn(vb_h, ks_sub)
            st = st_ref[di * heads + h]
            entering = [None] * N_SUB
            for c in (range(N_SUB) if fwd else reversed(range(N_SUB))):
                entering[c] = st * mid_scale[c:c + 1, sl]
                st = st * decay[c:c + 1, sl] + upd[:, c * d:(c + 1) * d] * upd_scale[c:c + 1, sl]
            st_ref[di * heads + h] = st
            if q_r is not None:
                qs_h = qs[:, sl]
                s = jnp.where(allowed, _dot_nt(qs_h, ks_h), 0.0).astype(BF16)
                qs_sub = jnp.concatenate([qs_h * cm_ref[slot, c] for c in range(N_SUB)], axis=1)
                st_cat = jnp.concatenate(entering, axis=1).astype(BF16)
                outs.append(_dot(s, vb_h) + _dot_nt(qs_sub, st_cat))
        return jnp.concatenate(outs, axis=1) if outs else None

    st_ref[...] = jnp.zeros_like(st_ref)

    def ctx_body(i, carry):
        scan_step(i, 0, 0, False, None, czf_ref, cv_ref)
        scan_step(n_ctx - 1 - i, 1, 0, False, None, czb_ref, cv_ref)
        return carry

    lax.fori_loop(0, n_ctx, ctx_body, 0)

    def lat_body(i, carry):
        store_rows(acc_ref.at[0], i, col_order, scan_step(i, 0, lat_slot, col_order, q_ref, zf_ref, v_ref))
        j = n_lat - 1 - i
        store_rows(acc_ref.at[1], j, col_order, scan_step(j, 1, lat_slot, col_order, q_ref, zb_ref, v_ref))
        return carry

    lax.fori_loop(0, n_lat, lat_body, 0, unroll=2)

    def readout_body(i, carry):
        blk = pl.ds(pl.multiple_of(i * SUPER, SUPER), SUPER)
        o = acc_ref[0, blk, :] + acc_ref[1, blk, :]
        g = g_ref[blk, :]
        res = []
        for h in range(heads):
            oh = o[:, hsl(h)]
            res.append(oh * lax.rsqrt(jnp.mean(oh * oh, axis=-1, keepdims=True) + EPS))
        o_ref[blk, :] = (jnp.concatenate(res, axis=1) * gain * (g * jax.nn.sigmoid(g))).astype(o_ref.dtype)
        return carry

    lax.fori_loop(0, n_lat, readout_body, 0)


def _hgrn(p_hg, p_ctx, lb_logits, gain, head0, n_heads, col_order, rows, name):
    b, seq, _ = p_hg.shape
    ctx_len = p_ctx.shape[1]
    assert seq % SUPER == 0 and ctx_len % SUPER == 0 and 2 * N_SUB == SUBLANES
    assert not col_order or (rows * SUBLANES == SUPER and GRID_W % SUBLANES == 0)
    hps = 2
    width = hps * HG_HEAD_DIM
    groups = n_heads // hps
    part_blocks = HG_WIDTH // width
    g0 = head0 // hps
    n_rows = lb_logits.shape[0]
    consts = _scan_constants((False, True) if col_order else (False,), rows)

    def slab(length, part):
        return pl.BlockSpec((None, length, width), lambda bi, gi: (bi, 0, part * part_blocks + g0 + gi))

    whole = lambda a: pl.BlockSpec(a.shape, lambda bi, gi: (0,) * a.ndim)
    kern = functools.partial(_hgrn_kernel, col_order=col_order, rows=rows)
    return pl.pallas_call(
        kern,
        grid=(b, groups),
        in_specs=[slab(seq, 0), slab(seq, 1), slab(seq, 2), slab(seq, 3), slab(seq, 4),
                  slab(ctx_len, 0), slab(ctx_len, 1), slab(ctx_len, 2),
                  pl.BlockSpec((n_rows, width), lambda bi, gi: (0, g0 + gi)),
                  pl.BlockSpec((1, width), lambda bi, gi: (0, g0 + gi))] + [whole(a) for a in consts],
        out_specs=pl.BlockSpec((None, seq, width), lambda bi, gi: (bi, 0, gi)),
        out_shape=jax.ShapeDtypeStruct((b, seq, n_heads * HG_HEAD_DIM), BF16),
        scratch_shapes=[pltpu.VMEM((2, seq, width), F32),
                        pltpu.VMEM((2 * hps, HG_HEAD_DIM, HG_HEAD_DIM), F32)],
        compiler_params=_cparams(("parallel", "parallel")),
        name=name,
    )(p_hg, p_hg, p_hg, p_hg, p_hg, p_ctx, p_ctx, p_ctx, lb_logits, gain, *consts)


def _dft_tables(seq):
    n = 2 * seq
    lo = GRID_W
    j = jnp.arange(n, dtype=jnp.int32)
    f = jnp.where(j < seq, j, j - seq)
    ang = lambda m: (m % n).astype(F32) * (2.0 * math.pi / n)
    a = ang(f[:, None] * (jnp.arange(seq // lo, dtype=jnp.int32) * lo)[None, :])
    b = ang(f[:, None] * jnp.arange(lo, dtype=jnp.int32)[None, :])
    ca, sa, cb, sb = jnp.cos(a), jnp.sin(a), jnp.cos(b), jnp.sin(b)
    t = jnp.arange(seq, dtype=jnp.int32)
    nyq = jnp.where(t % 2 == 0, 1.0, -1.0)

    cos_ft = (ca[:, :, None] * cb[:, None, :] - sa[:, :, None] * sb[:, None, :]).reshape(n, seq)
    sin_ft = (sa[:, :, None] * cb[:, None, :] + ca[:, :, None] * sb[:, None, :]).reshape(n, seq)
    jj = j[:, None]
    tab = jnp.where(jj < seq, cos_ft, jnp.where(jj == seq, nyq[None, :], sin_ft)).astype(BF16)
    tab = lax.optimization_barrier(tab)
    return tab, tab.T


def _filter_embedding(seq):
    pos = jnp.arange(seq, dtype=F32)[:, None]
    t = pos / max(seq - 1, 1)
    bands = jnp.linspace(1e-4, N_BANDS - 1, N_BANDS, dtype=F32)[None, :]
    ang = bands * (2.0 * math.pi) * pos / seq
    return jnp.concatenate([t, jnp.cos(ang), -jnp.sin(ang)], axis=-1)


def _filter_deltas(width):
    return jnp.abs(jnp.linspace(math.log(FILTER_TARGET) / SLOW_DECAY_PCT,
                                math.log(FILTER_TARGET) / FAST_DECAY_PCT, width, dtype=F32))


def _kspec_kernel(z_ref, w1_ref, b1_ref, fr_ref, w2_ref, b2_ref, w3f_ref, w3b_ref, dl_ref, f_ref, k_ref, h_ref):
    seq = z_ref.shape[0]
    ct = k_ref.shape[1]

    @pl.when(pl.program_id(0) == 0)
    def _():
        fr = fr_ref[...]
        h1 = jnp.sin(fr * (_dot3(z_ref[...], w1_ref[...]) + b1_ref[...]))
        h_ref[...] = jnp.sin(fr * (_dot3(h1, w2_ref[...]) + b2_ref[...]))

    h = h_ref[...]
    row = lax.broadcasted_iota(jnp.int32, (seq, ct), 0)
    t = row.astype(F32) / max(seq - 1, 1)
    window = jnp.exp(-t * dl_ref[...]) + FILTER_SHIFT
    hf = _dot3(h, w3f_ref[...]) * window
    hb = jnp.where(row == 0, 0.0, _dot3(h, w3b_ref[...]) * window)
    r = lax.rsqrt(jnp.sum(hf * hf, axis=0, keepdims=True) + jnp.sum(hb * hb, axis=0, keepdims=True))
    plus = ((hf + hb) * r).astype(BF16)
    minus = ((hf - hb) * r).astype(BF16)
    k_ref[0:seq, :] = _dot(f_ref[0:seq, :], plus)
    k_ref[seq:2 * seq, :] = _dot(f_ref[seq:2 * seq, :], minus)
    nyq = _dot(f_ref[seq:seq + SUBLANES, :], plus)
    first = lax.broadcasted_iota(jnp.int32, (SUBLANES, ct), 0) == 0
    k_ref[seq:seq + SUBLANES, :] = jnp.where(first, nyq, k_ref[seq:seq + SUBLANES, :])


def _kspec(z, w1, b1, freq, w2, b2, w3, deltas, fmat, ct):
    seq = z.shape[0]
    ch = deltas.shape[1]
    hid = w2.shape[0]
    nb = ch // ct
    full = lambda a: pl.BlockSpec(a.shape, lambda c: (0,) * a.ndim)
    return pl.pallas_call(
        _kspec_kernel,
        grid=(nb,),
        in_specs=[full(z), full(w1), full(b1), full(freq), full(w2), full(b2),
                  pl.BlockSpec((hid, ct), lambda c: (0, c)),
                  pl.BlockSpec((hid, ct), lambda c: (0, nb + c)),
                  pl.BlockSpec((1, ct), lambda c: (0, c)),
                  pl.BlockSpec(fmat.shape, lambda c: (0, 0), pipeline_mode=pl.Buffered(1))],
        out_specs=pl.BlockSpec((2 * seq, ct), lambda c: (0, c)),
        out_shape=jax.ShapeDtypeStruct((2 * seq, ch), F32),
        scratch_shapes=[pltpu.VMEM((seq, hid), F32)],
        compiler_params=_cparams(("arbitrary",)),
        name="hyena_filter_spectrum",
    )(z, w1, b1, freq, w2, b2, w3, w3, deltas, fmat)


def _short_conv(t, w, b):
    seq = t.shape[0]
    row = lax.broadcasted_iota(jnp.int32, t.shape, 0)
    prev = jnp.where(row == 0, 0.0, pltpu.roll(t, 1, 0))
    nxt = jnp.where(row == seq - 1, 0.0, pltpu.roll(t, seq - 1, 0))
    return prev * w[0:1] + t * w[1:2] + nxt * w[2:3] + b


def _hy_fwd_kernel(v_ref, x1_ref, wv_ref, wx1_ref, bv_ref, bx1_ref, f_ref, k_ref, z_ref, u_ref):
    seq = v_ref.shape[0]
    n = 2 * seq
    vc = _short_conv(v_ref[...], wv_ref[...], bv_ref[...])
    x1c = _short_conv(x1_ref[...], wx1_ref[...], bx1_ref[...])
    ub = (x1c * vc).astype(BF16)
    u_ref[...] = ub
    fs = seq // HY_SPLIT
    for r in range(HY_SPLIT):
        xa = _dot(f_ref[r * fs:(r + 1) * fs, :], ub)
        xb = _dot(f_ref[seq + r * fs:seq + (r + 1) * fs, :], ub)
        ka = k_ref[r * fs:(r + 1) * fs, :]
        kb = k_ref[seq + r * fs:seq + (r + 1) * fs, :]
        bb = xb * kb
        if r == 0:
            first = lax.broadcasted_iota(jnp.int32, xa.shape, 0) == 0
            w = jnp.where(first, 1.0 / n, 2.0 / n)
            za = (xa * ka - jnp.where(first, 0.0, bb)) * w
            zb = jnp.where(first, bb, xa * kb + xb * ka) * w
        else:
            za = (xa * ka - bb) * (2.0 / n)
            zb = (xa * kb + xb * ka) * (2.0 / n)
        z_ref[r * fs:(r + 1) * fs, :] = za.astype(z_ref.dtype)
        z_ref[seq + r * fs:seq + (r + 1) * fs, :] = zb.astype(z_ref.dtype)


def _hy_inv_kernel(z_ref, g_ref, u_ref, x0_ref, wx0_ref, bx0_ref, bias_ref, o_ref):
    seq = x0_ref.shape[0]
    x0c = _short_conv(x0_ref[...], wx0_ref[...], bx0_ref[...])
    ts = seq // HY_SPLIT
    for r in range(HY_SPLIT):
        rs = slice(r * ts, (r + 1) * ts)
        y = _dot(g_ref[rs, :], z_ref[...])
        o_ref[rs, :] = (x0c[rs] * (y + u_ref[rs, :].astype(F32) * bias_ref[...])).astype(o_ref.dtype)


def _hyena(p_hy, col0, conv_w, conv_b, kspec, bias, fmat, gmat, ct):
    b, seq, _ = p_hy.shape
    ch = bias.shape[1]
    nb = ch // ct
    taps = conv_w.shape[0]
    assert col0 % ct == 0
    act = lambda part: pl.BlockSpec((None, seq, ct), lambda c, bi: (bi, 0, col0 // ct + part * nb + c))
    cw = lambda part: pl.BlockSpec((taps, ct), lambda c, bi: (0, part * nb + c))
    cb = lambda part: pl.BlockSpec((1, ct), lambda c, bi: (0, part * nb + c))
    const = lambda a: pl.BlockSpec(a.shape, lambda c, bi: (0, 0), pipeline_mode=pl.Buffered(1))
    zspec = pl.BlockSpec((None, 2 * seq, ct), lambda c, bi: (bi, 0, c))
    uspec = pl.BlockSpec((None, seq, ct), lambda c, bi: (bi, 0, c))
    z, u = pl.pallas_call(
        _hy_fwd_kernel,
        grid=(nb, b),
        in_specs=[act(0), act(1), cw(0), cw(1), cb(0), cb(1), const(fmat),
                  pl.BlockSpec((2 * seq, ct), lambda c, bi: (0, c))],
        out_specs=[zspec, uspec],
        out_shape=[jax.ShapeDtypeStruct((b, 2 * seq, ch), BF16), jax.ShapeDtypeStruct((b, seq, ch), BF16)],
        compiler_params=_cparams(("parallel", "parallel")),
        name="hyena_forward_dft",
    )(p_hy, p_hy, conv_w, conv_w, conv_b, conv_b, fmat, kspec)
    return pl.pallas_call(
        _hy_inv_kernel,
        grid=(nb, b),
        in_specs=[zspec, const(gmat), uspec, act(2), cw(2), cb(2), pl.BlockSpec((1, ct), lambda c, bi: (0, c))],
        out_specs=pl.BlockSpec((None, seq, ct), lambda c, bi: (bi, 0, c)),
        out_shape=jax.ShapeDtypeStruct((b, seq, ch), BF16),
        compiler_params=_cparams(("parallel", "parallel")),
        name="hyena_inverse_dft",
    )(z, gmat, u, p_hy, conv_w, conv_b, bias)


def _wout_kernel(ar_ref, ac_ref, y_ref, x_ref, gate_ref, w_ref, o_ref):
    mix = jnp.concatenate([ar_ref[...], ac_ref[...], y_ref[...]], axis=1)
    o_ref[...] = x_ref[...] + gate_ref[...] * _dot(mix, w_ref[...])


def _wout(a_row, a_col, y, x2, gate, w, rows_per_mod, tm):
    t, d = x2.shape
    m = gate.shape[0]
    per = rows_per_mod // tm
    wr, wc, wy = a_row.shape[1], a_col.shape[1], y.shape[1]
    return pl.pallas_call(
        _wout_kernel,
        grid=(t // tm,),
        in_specs=[pl.BlockSpec((tm, wr), lambda i: (i, 0)),
                  pl.BlockSpec((tm, wc), lambda i: (i, 0)),
                  pl.BlockSpec((tm, wy), lambda i: (i, 0)),
                  pl.BlockSpec((tm, d), lambda i: (i, 0)),
                  pl.BlockSpec((None, 1, d), lambda i: (i // per, 0, 0)),
                  pl.BlockSpec(w.shape, lambda i: (0, 0))],
        out_specs=pl.BlockSpec((tm, d), lambda i: (i, 0)),
        out_shape=jax.ShapeDtypeStruct((t, d), F32),
        compiler_params=_cparams(("parallel",)),
        name="out_projection_residual",
    )(a_row, a_col, y, x2, gate.reshape(m, 1, d), w)


def _mlp_kernel(x_ref, g_ref, sh_ref, sc_ref, gate_ref, w1_ref, w2_ref, fg_ref, o_ref, u_ref):
    acc_ref = o_ref
    j = pl.program_id(1)
    last = pl.num_programs(1) - 1
    sub = u_ref.shape[0] // MLP_ROW_SPLIT

    def step(first, final):
        for r in range(MLP_ROW_SPLIT):
            rs = slice(r * sub, (r + 1) * sub)
            if first:
                y = _rms(x_ref[rs, :], g_ref[...])
                u_ref[rs, :] = (y * (1.0 + sc_ref[...]) + sh_ref[...]).astype(BF16)
            h = jnp.maximum(_dot(u_ref[rs, :], w1_ref[...]), 0.0)
            acc = _dot((h * h).astype(BF16), w2_ref[...])
            if not first:
                acc = acc_ref[rs, :] + acc
            if final:
                o_ref[rs, :] = _rms(x_ref[rs, :] + gate_ref[...] * acc, fg_ref[...])
            else:
                acc_ref[rs, :] = acc

    pl.when(j == 0)(functools.partial(step, True, False))
    pl.when((j > 0) & (j < last))(functools.partial(step, False, False))
    pl.when(j == last)(functools.partial(step, False, True))


def _mlp(x2, g, shift, scale, gate, w1, w2, fg, rows_per_mod, tm, tf):
    t, d = x2.shape
    dff = w1.shape[1]
    m = shift.shape[0]
    per = rows_per_mod // tm
    mod = pl.BlockSpec((None, 1, d), lambda i, j: (i // per, 0, 0))
    vec = pl.BlockSpec((1, d), lambda i, j: (0, 0))
    r3 = lambda a: a.reshape(m, 1, d)
    return pl.pallas_call(
        _mlp_kernel,
        grid=(t // tm, dff // tf),
        in_specs=[pl.BlockSpec((tm, d), lambda i, j: (i, 0)), vec, mod, mod, mod,
                  pl.BlockSpec((d, tf), lambda i, j: (0, j)),
                  pl.BlockSpec((tf, d), lambda i, j: (j, 0)),
                  vec],
        out_specs=pl.BlockSpec((tm, d), lambda i, j: (i, 0), pipeline_mode=pl.Buffered(1)),
        out_shape=jax.ShapeDtypeStruct((t, d), F32),
        scratch_shapes=[pltpu.VMEM((tm, d), BF16)],
        compiler_params=_cparams(("parallel", "arbitrary")),
        name="mlp_residual_final_norm",
    )(x2, g.reshape(1, d), r3(shift), r3(scale), r3(gate), w1, w2, fg.reshape(1, d))


def kernel(x, c, ctx, c_ctx, w_ada, b_ada, norm1_g, w_in, hgrn_lb_logits, hgrn_norm_g, hy_conv_w, hy_conv_b,
           flt_w1, flt_b1, flt_freq, flt_w2, flt_b2, flt_w3, hy_bias, w_out, norm2_g, w_mlp1, w_mlp2,
           final_norm_g):
    b, seq, d = x.shape
    ctx_len = ctx.shape[1]
    depth = w_ada.shape[0]
    assert depth == 1, "single-layer block"
    rows = seq // GRID_W
    hg_cols = 5 * HG_WIDTH
    hy_width = d - HG_WIDTH
    layer = 0

    pad = (-(b + 1)) % SUBLANES
    stacked = jnp.concatenate([c, c_ctx[None, :], jnp.zeros((pad, d), F32)], axis=0)
    mod = _ada(stacked, w_ada[layer], b_ada[layer])
    sh1, sc1, g1, sh2, sc2, g2 = [mod[:b, i * d:(i + 1) * d] for i in range(N_MOD)]
    csh1, csc1 = mod[b:b + 1, 0:d], mod[b:b + 1, d:2 * d]

    w_in_b = w_in[layer].astype(BF16)
    x2 = x.reshape(b * seq, d)
    ctx2 = ctx.reshape(b * ctx_len, d)
    p_lat = _norm_mod_matmul_pipelined(x2, norm1_g[layer], sh1, sc1, w_in_b, seq, 1024, 2048,
                                       "in_projection").reshape(b, seq, -1)
    p_ctx = _norm_mod_matmul(ctx2, norm1_g[layer], csh1, csc1, w_in_b[:, HG_WIDTH:4 * HG_WIDTH], b * ctx_len,
                             F32, min(1024, b * ctx_len), 1024, "in_projection_context")
    p_ctx = p_ctx.reshape(b, ctx_len, 3 * HG_WIDTH)

    lbl = hgrn_lb_logits.astype(F32).reshape(2 * (depth + 1), HG_WIDTH)
    gain = hgrn_norm_g[layer].reshape(1, HG_WIDTH)
    a_row = _hgrn(p_lat, p_ctx, lbl, gain, 0, HG_ROW_HEADS, False, rows, "hgrn_row_heads")
    a_col = _hgrn(p_lat, p_ctx, lbl, gain, HG_ROW_HEADS, HG_HEADS - HG_ROW_HEADS, True, rows, "hgrn_col_heads")

    fmat, gmat = _dft_tables(seq)
    emb = _filter_embedding(seq)
    kpad = (-emb.shape[1]) % LANES
    emb = jnp.pad(emb, ((0, 0), (0, kpad)))
    fw1 = jnp.pad(flt_w1[layer], ((0, kpad), (0, 0)))
    row = lambda a: a[layer].reshape(1, -1)
    kspec = _kspec(emb, fw1, row(flt_b1), row(flt_freq), flt_w2[layer], row(flt_b2), flt_w3[layer],
                   _filter_deltas(hy_width)[None, :], fmat, 256)
    y_lat = _hyena(p_lat, hg_cols, hy_conv_w[layer], hy_conv_b[layer].reshape(1, -1), kspec, row(hy_bias),
                   fmat, gmat, 256)

    x_mid = _wout(a_row.reshape(b * seq, -1), a_col.reshape(b * seq, -1), y_lat.reshape(b * seq, hy_width),
                  x2, g1, w_out[layer].astype(BF16), seq, 512)
    out = _mlp(x_mid, norm2_g[layer], sh2, sc2, g2, w_mlp1[layer].astype(BF16), w_mlp2[layer].astype(BF16),
               final_norm_g, seq, 512, 2048)
    return out.reshape(b, seq, d)
```

```python
import functools
import math

import jax
import jax.numpy as jnp
from jax import lax
from jax.experimental import pallas as pl
from jax.experimental.pallas import tpu as pltpu

GRID_W = 64
HG_WIDTH = 1024
HG_HEAD_DIM = 128
HG_HEADS = HG_WIDTH // HG_HEAD_DIM
HG_ROW_HEADS = HG_HEADS // 2
CHUNK = 64
N_BANDS = 16
FILTER_TARGET = 1e-2
FAST_DECAY_PCT = 0.3
SLOW_DECAY_PCT = 1.5
FILTER_SHIFT = 0.05
N_MOD = 6
EPS = 1e-6

F32 = jnp.float32
BF16 = jnp.bfloat16

V7X_VMEM_BYTES = 64 * 1024 * 1024
VMEM_LIMIT = 56 * 1024 * 1024
LANES = 128
SUBLANES = 8

SUPER = 256
N_SUB = SUPER // CHUNK

MLP_ROW_SPLIT = 2
HY_SPLIT = 2


def _cparams(sem, flags=None):
    return pltpu.CompilerParams(dimension_semantics=sem, vmem_limit_bytes=VMEM_LIMIT, flags=flags)


def _split_bf16(a):
    hi = a.astype(BF16)
    lo = (a - hi.astype(F32)).astype(BF16)
    return hi, lo


def _dot(a, b):
    return jnp.dot(a, b, preferred_element_type=F32)


def _dot_nt(a, b):
    return lax.dot_general(a, b, (((1,), (1,)), ((), ())), preferred_element_type=F32)


def _dot_tn(a, b):
    return lax.dot_general(a, b, (((0,), (0,)), ((), ())), preferred_element_type=F32)


def _dot3(a, b):
    ah, al = _split_bf16(a)
    bh, bl = _split_bf16(b)
    return _dot(ah, bh) + _dot(ah, bl) + _dot(al, bh)


def _rms(x, g):
    return x * lax.rsqrt(jnp.mean(x * x, axis=-1, keepdims=True) + EPS) * g


def _ilog2(n):
    assert n > 0 and n & (n - 1) == 0, n
    return n.bit_length() - 1


def _ada_kernel(s_ref, w_ref, b_ref, o_ref):
    s = s_ref[...]
    s = s * jax.nn.sigmoid(s)
    o_ref[...] = _dot3(s, w_ref[...]) + b_ref[...]


def _ada(stacked, w, b):
    r, d = stacked.shape
    n = w.shape[1]
    tn = 512
    return pl.pallas_call(
        _ada_kernel,
        grid=(n // tn,),
        in_specs=[pl.BlockSpec((r, d), lambda j: (0, 0)),
                  pl.BlockSpec((d, tn), lambda j: (0, j)),
                  pl.BlockSpec((1, tn), lambda j: (0, j))],
        out_specs=pl.BlockSpec((r, tn), lambda j: (0, j)),
        out_shape=jax.ShapeDtypeStruct((r, n), F32),
        compiler_params=_cparams(("parallel",)),
        name="ada_modulation",
    )(stacked, w, b.reshape(1, n))


def _nmm_kernel(x_ref, g_ref, sh_ref, sc_ref, w_ref, o_ref, u_ref):
    @pl.when(pl.program_id(1) == 0)
    def _():
        y = _rms(x_ref[...], g_ref[...])
        u_ref[...] = (y * (1.0 + sc_ref[...]) + sh_ref[...]).astype(BF16)

    o_ref[...] = _dot(u_ref[...], w_ref[...]).astype(o_ref.dtype)


def _norm_mod_matmul(x2, g, shift, scale, w, rows_per_mod, out_dtype, tm, tn, name):
    t, d = x2.shape
    n = w.shape[1]
    m = shift.shape[0]
    per = rows_per_mod // tm
    mod_map = lambda i, j: (i // per, 0, 0)
    return pl.pallas_call(
        _nmm_kernel,
        grid=(t // tm, n // tn),
        in_specs=[pl.BlockSpec((tm, d), lambda i, j: (i, 0)),
                  pl.BlockSpec((1, d), lambda i, j: (0, 0)),
                  pl.BlockSpec((None, 1, d), mod_map),
                  pl.BlockSpec((None, 1, d), mod_map),
                  pl.BlockSpec((d, tn), lambda i, j: (0, j))],
        out_specs=pl.BlockSpec((tm, tn), lambda i, j: (i, j)),
        out_shape=jax.ShapeDtypeStruct((t, n), out_dtype),
        scratch_shapes=[pltpu.VMEM((tm, d), BF16)],
        compiler_params=_cparams(("parallel", "arbitrary")),
        name=name,
    )(x2, g.reshape(1, d), shift.reshape(m, 1, d), scale.reshape(m, 1, d), w)


def _nmm_pipelined_kernel(*refs, n_side):
    xs_ref, g_ref, sh_ref, sc_ref, w_ref = refs[:5]
    side_in = refs[5:5 + n_side]
    o_ref = refs[5 + n_side]
    side_out = refs[6 + n_side:6 + 2 * n_side]
    ua_ref, ub_ref = refs[6 + 2 * n_side:]
    i = pl.program_id(0)
    j = pl.program_id(1)
    ts = xs_ref.shape[0]

    @pl.when((i == 0) & (j == 0))
    def _():
        ub_ref[...] = jnp.zeros_like(ub_ref)

    def body(fill_ref, use_ref):
        y = _rms(xs_ref[...], g_ref[...])
        fill_ref[pl.ds(pl.multiple_of(j * ts, ts), ts), :] = (y * (1.0 + sc_ref[...]) + sh_ref[...]).astype(BF16)
        for src, dst in zip(side_in, side_out):
            dst[...] = src[...].astype(BF16)
        o_ref[...] = _dot(use_ref[...], w_ref[...])

    pl.when(i % 2 == 0)(functools.partial(body, ua_ref, ub_ref))
    pl.when(i % 2 == 1)(functools.partial(body, ub_ref, ua_ref))


def _norm_mod_matmul_pipelined(x2, g, shift, scale, w, rows_per_mod, tm, tn, name, side_casts=()):
    t, d = x2.shape
    n = w.shape[1]
    m = shift.shape[0]
    per = rows_per_mod // tm
    nt, nc = t // tm, n // tn
    ts = tm // nc
    assert ts * nc == tm and ts % SUBLANES == 0
    steps = (nt + 1) * nc
    cur = lambda i: jnp.minimum(i, nt - 1)
    mod_map = lambda i, j: (cur(i) // per, 0, 0)

    side_specs = []
    for a in side_casts:
        rows = a.shape[0]
        br = next(r for r in range(2 * SUBLANES, rows + 1, 2 * SUBLANES) if rows % r == 0 and rows // r <= steps)
        nblk = rows // br
        side_specs.append(pl.BlockSpec((br, a.shape[1]),
                                       lambda i, j, nblk=nblk: (jnp.minimum(i * nc + j, nblk - 1), 0)))

    outs = pl.pallas_call(
        functools.partial(_nmm_pipelined_kernel, n_side=len(side_casts)),
        grid=(nt + 1, nc),
        in_specs=[pl.BlockSpec((ts, d), lambda i, j: (cur(i) * nc + j, 0)),
                  pl.BlockSpec((1, d), lambda i, j: (0, 0)),
                  pl.BlockSpec((None, 1, d), mod_map),
                  pl.BlockSpec((None, 1, d), mod_map),
                  pl.BlockSpec((d, tn), lambda i, j: (0, j))] + side_specs,
        out_specs=[pl.BlockSpec((tm, tn), lambda i, j: (jnp.maximum(i - 1, 0), jnp.where(i == 0, 0, j)))]
        + side_specs,
        out_shape=[jax.ShapeDtypeStruct((t, n), F32)] + [jax.ShapeDtypeStruct(a.shape, BF16) for a in side_casts],
        scratch_shapes=[pltpu.VMEM((tm, d), BF16), pltpu.VMEM((tm, d), BF16)],
        compiler_params=_cparams(("arbitrary", "arbitrary")),
        name=name,
    )(x2, g.reshape(1, d), shift.reshape(m, 1, d), scale.reshape(m, 1, d), w, *side_casts)
    return outs[0], tuple(outs[1:])


def _scan_pos(t, col_order, rows):
    if not col_order:
        return t
    cols = SUPER // rows
    return jnp.bitwise_and(t, cols - 1) * rows + jnp.right_shift(t, _ilog2(cols))


def _scan_constants(orders, rows):
    lc = _ilog2(CHUNK)
    half = CHUNK // 2
    dms, masks, rsels, cms = [], [], [], []
    for col_order in orders:
        pi = _scan_pos(lax.broadcasted_iota(jnp.int32, (SUPER, SUPER), 0), col_order, rows)
        pj = _scan_pos(lax.broadcasted_iota(jnp.int32, (SUPER, SUPER), 1), col_order, rows)
        same = jnp.right_shift(pi, lc) == jnp.right_shift(pj, lc)
        wj = jnp.bitwise_and(pj, CHUNK - 1)
        c8 = lax.broadcasted_iota(jnp.int32, (2 * N_SUB, SUPER), 0)
        p8 = _scan_pos(lax.broadcasted_iota(jnp.int32, (2 * N_SUB, SUPER), 1), col_order, rows)
        in_chunk = jnp.right_shift(p8, lc) == jnp.bitwise_and(c8, N_SUB - 1)
        w8 = jnp.bitwise_and(p8, CHUNK - 1)
        for fwd in (True, False):
            incl = same & ((pj <= pi) if fwd else (pj >= pi))
            upto_mid = same & ((wj < half) if fwd else (wj >= half))
            dms.append((incl.astype(F32) - upto_mid.astype(F32)).astype(BF16))
            masks.append(incl.astype(F32))
            sel = in_chunk & ((c8 >= N_SUB) | ((w8 < half) if fwd else (w8 >= half)))
            rsels.append(sel.astype(BF16))
        pr = jnp.right_shift(
            _scan_pos(lax.broadcasted_iota(jnp.int32, (SUPER, HG_HEAD_DIM), 0), col_order, rows), lc)
        cms.append(jnp.stack([(pr == c).astype(BF16) for c in range(N_SUB)]))
    return jnp.stack(dms), jnp.stack(masks), jnp.stack(rsels), jnp.stack(cms)


def _hgrn_kernel(q_ref, zf_ref, zb_ref, v_ref, g_ref, czf_ref, czb_ref, cv_ref, lbl_ref, gain_ref,
                 dm_ref, mask_ref, rsel_ref, cm_ref, o_ref, acc_ref, st_ref, *, col_order, rows):
    seq, width = q_ref.shape
    ctx_len = cv_ref.shape[0]
    heads = width // HG_HEAD_DIM
    n_lat = seq // SUPER
    n_ctx = ctx_len // SUPER
    n_slots = lbl_ref.shape[0] // 2
    cols = SUPER // rows
    d = HG_HEAD_DIM
    lat_slot = cm_ref.shape[0] - 1

    def lower_bound(direction):
        lg = lbl_ref[direction * n_slots:(direction + 1) * n_slots, :]
        e = jnp.exp(lg - jnp.max(lg, axis=0, keepdims=True))
        return e[0:1] / jnp.sum(e, axis=0, keepdims=True)

    lbs = (lower_bound(0), lower_bound(1))
    gain = gain_ref[...]

    def row_starts(sc, order):
        if not order:
            return [(pl.multiple_of(sc * SUPER, SUPER), SUPER)]
        return [(pl.multiple_of(r * GRID_W + sc * cols, cols), cols) for r in range(rows)]

    def load_rows(ref, sc, order):
        parts = [ref[pl.ds(s, n), :] for s, n in row_starts(sc, order)]
        return parts[0] if len(parts) == 1 else jnp.concatenate(parts, axis=0)

    def store_rows(ref, sc, order, val):
        off = 0
        for s, n in row_starts(sc, order):
            ref[pl.ds(s, n), :] = val[off:off + n]
            off += n

    def hsl(h):
        return slice(h * d, (h + 1) * d)

    def scan_step(sc, di, slot, order, q_r, z_r, v_r):
        fwd = di == 0
        k_idx = 2 * slot + di
        z = load_rows(z_r, sc, order)
        lb = lbs[di]
        f = lb + (1.0 - lb) * jax.nn.sigmoid(z)
        k = 1.0 - f
        lf_hi, lf_lo = _split_bf16(jnp.log(f))
        dm = dm_ref[k_idx]
        x1 = _dot(dm, lf_hi) + _dot(dm, lf_lo)
        rsel = rsel_ref[k_idx]
        ref_sums = _dot(rsel, lf_hi) + _dot(rsel, lf_lo)
        a_mid = ref_sums[0:N_SUB]
        a_end = ref_sums[N_SUB:2 * N_SUB]
        upd_scale = jnp.exp(a_end - a_mid)
        decay = jnp.exp(a_end)
        mid_scale = jnp.exp(a_mid)
        ks = (k * jnp.exp(-x1)).astype(BF16)
        vb = load_rows(v_r, sc, order).astype(BF16)
        if q_r is not None:
            qs = (load_rows(q_r, sc, order) * jnp.exp(x1)).astype(BF16)
            allowed = mask_ref[k_idx] != 0.0
        outs = []
        for h in range(heads):
            sl = hsl(h)
            ks_h = ks[:, sl]
            vb_h = vb[:, sl]
            ks_sub = jnp.concatenate([ks_h * cm_ref[slot, c] for c in range(N_SUB)], axis=1)
            upd = _dot_tn(vb_h, ks_sub)
            st = st_ref[di * heads + h]
            entering = [None] * N_SUB
            for c in (range(N_SUB) if fwd else reversed(range(N_SUB))):
                entering[c] = st * mid_scale[c:c + 1, sl]
                st = st * decay[c:c + 1, sl] + upd[:, c * d:(c + 1) * d] * upd_scale[c:c + 1, sl]
            st_ref[di * heads + h] = st
            if q_r is not None:
                qs_h = qs[:, sl]
                s = jnp.where(allowed, _dot_nt(qs_h, ks_h), 0.0).astype(BF16)
                qs_sub = jnp.concatenate([qs_h * cm_ref[slot, c] for c in range(N_SUB)], axis=1)
                st_cat = jnp.concatenate(entering, axis=1).astype(BF16)
                outs.append(_dot(s, vb_h) + _dot_nt(qs_sub, st_cat))
        return jnp.concatenate(outs, axis=1) if outs else None

    st_ref[...] = jnp.zeros_like(st_ref)

    def ctx_body(i, carry):
        scan_step(i, 0, 0, False, None, czf_ref, cv_ref)
        scan_step(n_ctx - 1 - i, 1, 0, False, None, czb_ref, cv_ref)
        return carry

    lax.fori_loop(0, n_ctx, ctx_body, 0)

    def lat_body(i, carry):
        store_rows(acc_ref.at[0], i, col_order, scan_step(i, 0, lat_slot, col_order, q_ref, zf_ref, v_ref))
        j = n_lat - 1 - i
        store_rows(acc_ref.at[1], j, col_order, scan_step(j, 1, lat_slot, col_order, q_ref, zb_ref, v_ref))
        return carry

    lax.fori_loop(0, n_lat, lat_body, 0, unroll=2)

    def readout_body(i, carry):
        blk = pl.ds(pl.multiple_of(i * SUPER, SUPER), SUPER)
        o = acc_ref[0, blk, :] + acc_ref[1, blk, :]
        g = g_ref[blk, :]
        res = []
        for h in range(heads):
            oh = o[:, hsl(h)]
            res.append(oh * lax.rsqrt(jnp.mean(oh * oh, axis=-1, keepdims=True) + EPS))
        o_ref[blk, :] = (jnp.concatenate(res, axis=1) * gain * (g * jax.nn.sigmoid(g))).astype(o_ref.dtype)
        return carry

    lax.fori_loop(0, n_lat, readout_body, 0)


def _hgrn(p_hg, p_ctx, lb_logits, gain, head0, n_heads, col_order, rows, name):
    b, seq, _ = p_hg.shape
    ctx_len = p_ctx.shape[1]
    assert seq % SUPER == 0 and ctx_len % SUPER == 0 and 2 * N_SUB == SUBLANES
    assert not col_order or (rows * SUBLANES == SUPER and GRID_W % SUBLANES == 0)
    hps = 2
    width = hps * HG_HEAD_DIM
    groups = n_heads // hps
    part_blocks = HG_WIDTH // width
    g0 = head0 // hps
    n_rows = lb_logits.shape[0]
    consts = _scan_constants((False, True) if col_order else (False,), rows)

    def slab(length, part):
        return pl.BlockSpec((None, length, width), lambda bi, gi: (bi, 0, part * part_blocks + g0 + gi))

    whole = lambda a: pl.BlockSpec(a.shape, lambda bi, gi: (0,) * a.ndim)
    kern = functools.partial(_hgrn_kernel, col_order=col_order, rows=rows)
    return pl.pallas_call(
        kern,
        grid=(b, groups),
        in_specs=[slab(seq, 0), slab(seq, 1), slab(seq, 2), slab(seq, 3), slab(seq, 4),
                  slab(ctx_len, 0), slab(ctx_len, 1), slab(ctx_len, 2),
                  pl.BlockSpec((n_rows, width), lambda bi, gi: (0, g0 + gi)),
                  pl.BlockSpec((1, width), lambda bi, gi: (0, g0 + gi))] + [whole(a) for a in consts],
        out_specs=pl.BlockSpec((None, seq, width), lambda bi, gi: (bi, 0, gi)),
        out_shape=jax.ShapeDtypeStruct((b, seq, n_heads * HG_HEAD_DIM), BF16),
        scratch_shapes=[pltpu.VMEM((2, seq, width), F32),
                        pltpu.VMEM((2 * hps, HG_HEAD_DIM, HG_HEAD_DIM), F32)],
        compiler_params=_cparams(("parallel", "parallel")),
        name=name,
    )(p_hg, p_hg, p_hg, p_hg, p_hg, p_ctx, p_ctx, p_ctx, lb_logits, gain, *consts)


def _dft_tables(seq):
    n = 2 * seq
    lo = GRID_W
    j = jnp.arange(n, dtype=jnp.int32)
    f = jnp.where(j < seq, j, j - seq)
    ang = lambda m: (m % n).astype(F32) * (2.0 * math.pi / n)
    a = ang(f[:, None] * (jnp.arange(seq // lo, dtype=jnp.int32) * lo)[None, :])
    b = ang(f[:, None] * jnp.arange(lo, dtype=jnp.int32)[None, :])
    ca, sa, cb, sb = jnp.cos(a), jnp.sin(a), jnp.cos(b), jnp.sin(b)
    t = jnp.arange(seq, dtype=jnp.int32)
    nyq = jnp.where(t % 2 == 0, 1.0, -1.0)

    cos_ft = (ca[:, :, None] * cb[:, None, :] - sa[:, :, None] * sb[:, None, :]).reshape(n, seq)
    sin_ft = (sa[:, :, None] * cb[:, None, :] + ca[:, :, None] * sb[:, None, :]).reshape(n, seq)
    jj = j[:, None]
    tab = jnp.where(jj < seq, cos_ft, jnp.where(jj == seq, nyq[None, :], sin_ft)).astype(BF16)
    tab = lax.optimization_barrier(tab)
    return tab, tab.T


def _filter_embedding(seq):
    pos = jnp.arange(seq, dtype=F32)[:, None]
    t = pos / max(seq - 1, 1)
    bands = jnp.linspace(1e-4, N_BANDS - 1, N_BANDS, dtype=F32)[None, :]
    ang = bands * (2.0 * math.pi) * pos / seq
    return jnp.concatenate([t, jnp.cos(ang), -jnp.sin(ang)], axis=-1)


def _filter_deltas(width):
    return jnp.abs(jnp.linspace(math.log(FILTER_TARGET) / SLOW_DECAY_PCT,
                                math.log(FILTER_TARGET) / FAST_DECAY_PCT, width, dtype=F32))


def _kspec_kernel(z_ref, w1_ref, b1_ref, fr_ref, w2_ref, b2_ref, w3f_ref, w3b_ref, dl_ref, f_ref, k_ref, h_ref):
    seq = z_ref.shape[0]
    ct = k_ref.shape[1]

    @pl.when(pl.program_id(0) == 0)
    def _():
        fr = fr_ref[...]
        h1 = jnp.sin(fr * (_dot3(z_ref[...], w1_ref[...]) + b1_ref[...]))
        h_ref[...] = jnp.sin(fr * (_dot3(h1, w2_ref[...]) + b2_ref[...]))

    h = h_ref[...]
    row = lax.broadcasted_iota(jnp.int32, (seq, ct), 0)
    t = row.astype(F32) / max(seq - 1, 1)
    window = jnp.exp(-t * dl_ref[...]) + FILTER_SHIFT
    hf = _dot3(h, w3f_ref[...]) * window
    hb = jnp.where(row == 0, 0.0, _dot3(h, w3b_ref[...]) * window)
    r = lax.rsqrt(jnp.sum(hf * hf, axis=0, keepdims=True) + jnp.sum(hb * hb, axis=0, keepdims=True))
    plus = ((hf + hb) * r).astype(BF16)
    minus = ((hf - hb) * r).astype(BF16)
    k_ref[0:seq, :] = _dot(f_ref[0:seq, :], plus)
    k_ref[seq:2 * seq, :] = _dot(f_ref[seq:2 * seq, :], minus)
    nyq = _dot(f_ref[seq:seq + SUBLANES, :], plus)
    first = lax.broadcasted_iota(jnp.int32, (SUBLANES, ct), 0) == 0
    k_ref[seq:seq + SUBLANES, :] = jnp.where(first, nyq, k_ref[seq:seq + SUBLANES, :])


def _kspec(z, w1, b1, freq, w2, b2, w3, deltas, fmat, ct):
    seq = z.shape[0]
    ch = deltas.shape[1]
    hid = w2.shape[0]
    nb = ch // ct
    full = lambda a: pl.BlockSpec(a.shape, lambda c: (0,) * a.ndim)
    return pl.pallas_call(
        _kspec_kernel,
        grid=(nb,),
        in_specs=[full(z), full(w1), full(b1), full(freq), full(w2), full(b2),
                  pl.BlockSpec((hid, ct), lambda c: (0, c)),
                  pl.BlockSpec((hid, ct), lambda c: (0, nb + c)),
                  pl.BlockSpec((1, ct), lambda c: (0, c)),
                  pl.BlockSpec(fmat.shape, lambda c: (0, 0), pipeline_mode=pl.Buffered(1))],
        out_specs=pl.BlockSpec((2 * seq, ct), lambda c: (0, c)),
        out_shape=jax.ShapeDtypeStruct((2 * seq, ch), F32),
        scratch_shapes=[pltpu.VMEM((seq, hid), F32)],
        compiler_params=_cparams(("arbitrary",)),
        name="hyena_filter_spectrum",
    )(z, w1, b1, freq, w2, b2, w3, w3, deltas, fmat)


def _short_conv(t, w, b):
    seq = t.shape[0]
    row = lax.broadcasted_iota(jnp.int32, t.shape, 0)
    prev = jnp.where(row == 0, 0.0, pltpu.roll(t, 1, 0))
    nxt = jnp.where(row == seq - 1, 0.0, pltpu.roll(t, seq - 1, 0))
    return prev * w[0:1] + t * w[1:2] + nxt * w[2:3] + b


def _hy_fwd_kernel(v_ref, x1_ref, wv_ref, wx1_ref, bv_ref, bx1_ref, f_ref, k_ref, z_ref, u_ref):
    seq = v_ref.shape[0]
    n = 2 * seq
    vc = _short_conv(v_ref[...], wv_ref[...], bv_ref[...])
    x1c = _short_conv(x1_ref[...], wx1_ref[...], bx1_ref[...])
    ub = (x1c * vc).astype(BF16)
    u_ref[...] = ub
    fs = seq // HY_SPLIT
    for r in range(HY_SPLIT):
        xa = _dot(f_ref[r * fs:(r + 1) * fs, :], ub)
        xb = _dot(f_ref[seq + r * fs:seq + (r + 1) * fs, :], ub)
        ka = k_ref[r * fs:(r + 1) * fs, :]
        kb = k_ref[seq + r * fs:seq + (r + 1) * fs, :]
        bb = xb * kb
        if r == 0:
            first = lax.broadcasted_iota(jnp.int32, xa.shape, 0) == 0
            w = jnp.where(first, 1.0 / n, 2.0 / n)
            za = (xa * ka - jnp.where(first, 0.0, bb)) * w
            zb = jnp.where(first, bb, xa * kb + xb * ka) * w
        else:
            za = (xa * ka - bb) * (2.0 / n)
            zb = (xa * kb + xb * ka) * (2.0 / n)
        z_ref[r * fs:(r + 1) * fs, :] = za.astype(z_ref.dtype)
        z_ref[seq + r * fs:seq + (r + 1) * fs, :] = zb.astype(z_ref.dtype)


def _hy_inv_kernel(z_ref, g_ref, u_ref, x0_ref, wx0_ref, bx0_ref, bias_ref, o_ref):
    seq = x0_ref.shape[0]
    x0c = _short_conv(x0_ref[...], wx0_ref[...], bx0_ref[...])
    ts = seq // HY_SPLIT
    for r in range(HY_SPLIT):
        rs = slice(r * ts, (r + 1) * ts)
        y = _dot(g_ref[rs, :], z_ref[...])
        o_ref[rs, :] = (x0c[rs] * (y + u_ref[rs, :].astype(F32) * bias_ref[...])).astype(o_ref.dtype)


def _hyena(p_hy, col0, conv_w, conv_b, kspec, bias, fmat, gmat, ct):
    b, seq, _ = p_hy.shape
    ch = bias.shape[1]
    nb = ch // ct
    taps = conv_w.shape[0]
    assert col0 % ct == 0
    act = lambda part: pl.BlockSpec((None, seq, ct), lambda c, bi: (bi, 0, col0 // ct + part * nb + c))
    cw = lambda part: pl.BlockSpec((taps, ct), lambda c, bi: (0, part * nb + c))
    cb = lambda part: pl.BlockSpec((1, ct), lambda c, bi: (0, part * nb + c))
    const = lambda a: pl.BlockSpec(a.shape, lambda c, bi: (0, 0), pipeline_mode=pl.Buffered(1))
    zspec = pl.BlockSpec((None, 2 * seq, ct), lambda c, bi: (bi, 0, c))
    uspec = pl.BlockSpec((None, seq, ct), lambda c, bi: (bi, 0, c))
    z, u = pl.pallas_call(
        _hy_fwd_kernel,
        grid=(nb, b),
        in_specs=[act(0), act(1), cw(0), cw(1), cb(0), cb(1), const(fmat),
                  pl.BlockSpec((2 * seq, ct), lambda c, bi: (0, c))],
        out_specs=[zspec, uspec],
        out_shape=[jax.ShapeDtypeStruct((b, 2 * seq, ch), BF16), jax.ShapeDtypeStruct((b, seq, ch), BF16)],
        compiler_params=_cparams(("parallel", "parallel")),
        name="hyena_forward_dft",
    )(p_hy, p_hy, conv_w, conv_w, conv_b, conv_b, fmat, kspec)
    return pl.pallas_call(
        _hy_inv_kernel,
        grid=(nb, b),
        in_specs=[zspec, const(gmat), uspec, act(2), cw(2), cb(2), pl.BlockSpec((1, ct), lambda c, bi: (0, c))],
        out_specs=pl.BlockSpec((None, seq, ct), lambda c, bi: (bi, 0, c)),
        out_shape=jax.ShapeDtypeStruct((b, seq, ch), BF16),
        compiler_params=_cparams(("parallel", "parallel")),
        name="hyena_inverse_dft",
    )(z, gmat, u, p_hy, conv_w, conv_b, bias)


def _wout_kernel(ar_ref, ac_ref, y_ref, x_ref, gate_ref, w_ref, o_ref):
    mix = jnp.concatenate([ar_ref[...], ac_ref[...], y_ref[...]], axis=1)
    o_ref[...] = x_ref[...] + gate_ref[...] * _dot(mix, w_ref[...])


def _wout(a_row, a_col, y, x2, gate, w, rows_per_mod, tm):
    t, d = x2.shape
    m = gate.shape[0]
    per = rows_per_mod // tm
    wr, wc, wy = a_row.shape[1], a_col.shape[1], y.shape[1]
    return pl.pallas_call(
        _wout_kernel,
        grid=(t // tm,),
        in_specs=[pl.BlockSpec((tm, wr), lambda i: (i, 0)),
                  pl.BlockSpec((tm, wc), lambda i: (i, 0)),
                  pl.BlockSpec((tm, wy), lambda i: (i, 0)),
                  pl.BlockSpec((tm, d), lambda i: (i, 0)),
                  pl.BlockSpec((None, 1, d), lambda i: (i // per, 0, 0)),
                  pl.BlockSpec(w.shape, lambda i: (0, 0))],
        out_specs=pl.BlockSpec((tm, d), lambda i: (i, 0)),
        out_shape=jax.ShapeDtypeStruct((t, d), F32),
        compiler_params=_cparams(("parallel",)),
        name="out_projection_residual",
    )(a_row, a_col, y, x2, gate.reshape(m, 1, d), w)


def _mlp_kernel(x_ref, g_ref, sh_ref, sc_ref, gate_ref, w1_ref, w2_ref, fg_ref, o_ref, u_ref, acc_ref):
    j = pl.program_id(1)
    last = pl.num_programs(1) - 1
    sub = u_ref.shape[0] // MLP_ROW_SPLIT

    def step(first, final):
        for r in range(MLP_ROW_SPLIT):
            rs = slice(r * sub, (r + 1) * sub)
            if first:
                y = _rms(x_ref[rs, :], g_ref[...])
                u_ref[rs, :] = (y * (1.0 + sc_ref[...]) + sh_ref[...]).astype(BF16)
            h = jnp.maximum(_dot(u_ref[rs, :], w1_ref[...]), 0.0)
            acc = _dot((h * h).astype(BF16), w2_ref[...])
            if not first:
                acc = acc_ref[rs, :] + acc
            if final:
                o_ref[rs, :] = _rms(x_ref[rs, :] + gate_ref[...] * acc, fg_ref[...])
            else:
                acc_ref[rs, :] = acc

    pl.when(j == 0)(functools.partial(step, True, False))
    pl.when((j > 0) & (j < last))(functools.partial(step, False, False))
    pl.when(j == last)(functools.partial(step, False, True))


def _mlp(x2, g, shift, scale, gate, w1, w2, fg, rows_per_mod, tm, tf):
    t, d = x2.shape
    dff = w1.shape[1]
    m = shift.shape[0]
    per = rows_per_mod // tm
    mod = pl.BlockSpec((None, 1, d), lambda i, j: (i // per, 0, 0))
    vec = pl.BlockSpec((1, d), lambda i, j: (0, 0))
    r3 = lambda a: a.reshape(m, 1, d)
    return pl.pallas_call(
        _mlp_kernel,
        grid=(t // tm, dff // tf),
        in_specs=[pl.BlockSpec((tm, d), lambda i, j: (i, 0)), vec, mod, mod, mod,
                  pl.BlockSpec((d, tf), lambda i, j: (0, j)),
                  pl.BlockSpec((tf, d), lambda i, j: (j, 0)),
                  vec],
        out_specs=pl.BlockSpec((tm, d), lambda i, j: (i, 0)),
        out_shape=jax.ShapeDtypeStruct((t, d), F32),
        scratch_shapes=[pltpu.VMEM((tm, d), BF16), pltpu.VMEM((tm, d), F32)],
        compiler_params=_cparams(("parallel", "arbitrary")),
        name="mlp_residual_final_norm",
    )(x2, g.reshape(1, d), r3(shift), r3(scale), r3(gate), w1, w2, fg.reshape(1, d))


def kernel(x, c, ctx, c_ctx, w_ada, b_ada, norm1_g, w_in, hgrn_lb_logits, hgrn_norm_g, hy_conv_w, hy_conv_b,
           flt_w1, flt_b1, flt_freq, flt_w2, flt_b2, flt_w3, hy_bias, w_out, norm2_g, w_mlp1, w_mlp2,
           final_norm_g):
    b, seq, d = x.shape
    ctx_len = ctx.shape[1]
    depth = w_ada.shape[0]
    assert depth == 1, "single-layer block"
    rows = seq // GRID_W
    hg_cols = 5 * HG_WIDTH
    hy_width = d - HG_WIDTH
    layer = 0

    pad = (-(b + 1)) % SUBLANES
    stacked = jnp.concatenate([c, c_ctx[None, :], jnp.zeros((pad, d), F32)], axis=0)
    mod = _ada(stacked, w_ada[layer], b_ada[layer])
    sh1, sc1, g1, sh2, sc2, g2 = [mod[:b, i * d:(i + 1) * d] for i in range(N_MOD)]
    csh1, csc1 = mod[b:b + 1, 0:d], mod[b:b + 1, d:2 * d]

    w_in_b = w_in[layer].astype(BF16)
    x2 = x.reshape(b * seq, d)
    ctx2 = ctx.reshape(b * ctx_len, d)
    p_lat, (w_out_b, w_mlp1_b, w_mlp2_b) = _norm_mod_matmul_pipelined(
        x2, norm1_g[layer], sh1, sc1, w_in_b, seq, 1024, 2048, "in_projection",
        side_casts=(w_out[layer], w_mlp1[layer], w_mlp2[layer]))
    p_lat = p_lat.reshape(b, seq, -1)
    p_ctx = _norm_mod_matmul(ctx2, norm1_g[layer], csh1, csc1, w_in_b[:, HG_WIDTH:4 * HG_WIDTH], b * ctx_len,
                             F32, min(1024, b * ctx_len), 1024, "in_projection_context")
    p_ctx = p_ctx.reshape(b, ctx_len, 3 * HG_WIDTH)

    lbl = hgrn_lb_logits.astype(F32).reshape(2 * (depth + 1), HG_WIDTH)
    gain = hgrn_norm_g[layer].reshape(1, HG_WIDTH)
    a_row = _hgrn(p_lat, p_ctx, lbl, gain, 0, HG_ROW_HEADS, False, rows, "hgrn_row_heads")
    a_col = _hgrn(p_lat, p_ctx, lbl, gain, HG_ROW_HEADS, HG_HEADS - HG_ROW_HEADS, True, rows, "hgrn_col_heads")

    fmat, gmat = _dft_tables(seq)
    emb = _filter_embedding(seq)
    kpad = (-emb.shape[1]) % LANES
    emb = jnp.pad(emb, ((0, 0), (0, kpad)))
    fw1 = jnp.pad(flt_w1[layer], ((0, kpad), (0, 0)))
    row = lambda a: a[layer].reshape(1, -1)
    kspec = _kspec(emb, fw1, row(flt_b1), row(flt_freq), flt_w2[layer], row(flt_b2), flt_w3[layer],
                   _filter_deltas(hy_width)[None, :], fmat, 256)
    y_lat = _hyena(p_lat, hg_cols, hy_conv_w[layer], hy_conv_b[layer].reshape(1, -1), kspec, row(hy_bias),
                   fmat, gmat, 256)

    x_mid = _wout(a_row.reshape(b * seq, -1), a_col.reshape(b * seq, -1), y_lat.reshape(b * seq, hy_width),
                  x2, g1, w_out_b, seq, 512)
    out = _mlp(x_mid, norm2_g[layer], sh2, sc2, g2, w_mlp1_b, w_mlp2_b, final_norm_g, seq, 512, 1024)
    return out.reshape(b, seq, d)
```

```python
import functools
import math

import jax
import jax.numpy as jnp
from jax import lax
from jax.experimental import pallas as pl
from jax.experimental.pallas import tpu as pltpu

GRID_W = 64
HG_WIDTH = 1024
HG_HEAD_DIM = 128
HG_HEADS = HG_WIDTH // HG_HEAD_DIM
HG_ROW_HEADS = HG_HEADS // 2
CHUNK = 64
N_BANDS = 16
FILTER_TARGET = 1e-2
FAST_DECAY_PCT = 0.3
SLOW_DECAY_PCT = 1.5
FILTER_SHIFT = 0.05
N_MOD = 6
EPS = 1e-6

F32 = jnp.float32
BF16 = jnp.bfloat16

V7X_VMEM_BYTES = 64 * 1024 * 1024
VMEM_LIMIT = 56 * 1024 * 1024
LANES = 128
SUBLANES = 8

SUPER = 256
N_SUB = SUPER // CHUNK

MLP_ROW_SPLIT = 2
HY_SPLIT = 2


def _cparams(sem, flags=None):
    return pltpu.CompilerParams(dimension_semantics=sem, vmem_limit_bytes=VMEM_LIMIT, flags=flags)


def _split_bf16(a):
    hi = a.astype(BF16)
    lo = (a - hi.astype(F32)).astype(BF16)
    return hi, lo


def _dot(a, b):
    return jnp.dot(a, b, preferred_element_type=F32)


def _dot_nt(a, b):
    return lax.dot_general(a, b, (((1,), (1,)), ((), ())), preferred_element_type=F32)


def _dot_tn(a, b):
    return lax.dot_general(a, b, (((0,), (0,)), ((), ())), preferred_element_type=F32)


def _dot3(a, b):
    ah, al = _split_bf16(a)
    bh, bl = _split_bf16(b)
    return _dot(ah, bh) + _dot(ah, bl) + _dot(al, bh)


def _rms(x, g):
    return x * lax.rsqrt(jnp.mean(x * x, axis=-1, keepdims=True) + EPS) * g


def _ilog2(n):
    assert n > 0 and n & (n - 1) == 0, n
    return n.bit_length() - 1


def _ada_kernel(s_ref, w_ref, b_ref, o_ref):
    s = s_ref[...]
    s = s * jax.nn.sigmoid(s)
    o_ref[...] = _dot3(s, w_ref[...]) + b_ref[...]


def _ada(stacked, w, b):
    r, d = stacked.shape
    n = w.shape[1]
    tn = 512
    return pl.pallas_call(
        _ada_kernel,
        grid=(n // tn,),
        in_specs=[pl.BlockSpec((r, d), lambda j: (0, 0)),
                  pl.BlockSpec((d, tn), lambda j: (0, j)),
                  pl.BlockSpec((1, tn), lambda j: (0, j))],
        out_specs=pl.BlockSpec((r, tn), lambda j: (0, j)),
        out_shape=jax.ShapeDtypeStruct((r, n), F32),
        compiler_params=_cparams(("parallel",)),
        name="ada_modulation",
    )(stacked, w, b.reshape(1, n))


def _nmm_kernel(x_ref, g_ref, sh_ref, sc_ref, w_ref, o_ref, u_ref):
    @pl.when(pl.program_id(1) == 0)
    def _():
        y = _rms(x_ref[...], g_ref[...])
        u_ref[...] = (y * (1.0 + sc_ref[...]) + sh_ref[...]).astype(BF16)

    o_ref[...] = _dot(u_ref[...], w_ref[...]).astype(o_ref.dtype)


def _norm_mod_matmul(x2, g, shift, scale, w, rows_per_mod, out_dtype, tm, tn, name):
    t, d = x2.shape
    n = w.shape[1]
    m = shift.shape[0]
    per = rows_per_mod // tm
    mod_map = lambda i, j: (i // per, 0, 0)
    return pl.pallas_call(
        _nmm_kernel,
        grid=(t // tm, n // tn),
        in_specs=[pl.BlockSpec((tm, d), lambda i, j: (i, 0)),
                  pl.BlockSpec((1, d), lambda i, j: (0, 0)),
                  pl.BlockSpec((None, 1, d), mod_map),
                  pl.BlockSpec((None, 1, d), mod_map),
                  pl.BlockSpec((d, tn), lambda i, j: (0, j))],
        out_specs=pl.BlockSpec((tm, tn), lambda i, j: (i, j)),
        out_shape=jax.ShapeDtypeStruct((t, n), out_dtype),
        scratch_shapes=[pltpu.VMEM((tm, d), BF16)],
        compiler_params=_cparams(("parallel", "arbitrary")),
        name=name,
    )(x2, g.reshape(1, d), shift.reshape(m, 1, d), scale.reshape(m, 1, d), w)


def _nmm_pipelined_kernel(*refs, n_side):
    xs_ref, g_ref, sh_ref, sc_ref, w_ref = refs[:5]
    side_in = refs[5:5 + n_side]
    o_ref = refs[5 + n_side]
    side_out = refs[6 + n_side:6 + 2 * n_side]
    ua_ref, ub_ref = refs[6 + 2 * n_side:]
    i = pl.program_id(0)
    j = pl.program_id(1)
    ts = xs_ref.shape[0]

    @pl.when((i == 0) & (j == 0))
    def _():
        ub_ref[...] = jnp.zeros_like(ub_ref)

    def body(fill_ref, use_ref):
        y = _rms(xs_ref[...], g_ref[...])
        fill_ref[pl.ds(pl.multiple_of(j * ts, ts), ts), :] = (y * (1.0 + sc_ref[...]) + sh_ref[...]).astype(BF16)
        for src, dst in zip(side_in, side_out):
            dst[...] = src[...].astype(BF16)
        o_ref[...] = _dot(use_ref[...], w_ref[...])

    pl.when(i % 2 == 0)(functools.partial(body, ua_ref, ub_ref))
    pl.when(i % 2 == 1)(functools.partial(body, ub_ref, ua_ref))


def _norm_mod_matmul_pipelined(x2, g, shift, scale, w, rows_per_mod, tm, tn, name, side_casts=()):
    t, d = x2.shape
    n = w.shape[1]
    m = shift.shape[0]
    per = rows_per_mod // tm
    nt, nc = t // tm, n // tn
    ts = tm // nc
    assert ts * nc == tm and ts % SUBLANES == 0
    steps = (nt + 1) * nc
    cur = lambda i: jnp.minimum(i, nt - 1)
    mod_map = lambda i, j: (cur(i) // per, 0, 0)

    side_specs = []
    for a in side_casts:
        rows = a.shape[0]
        br = next(r for r in range(2 * SUBLANES, rows + 1, 2 * SUBLANES) if rows % r == 0 and rows // r <= steps)
        nblk = rows // br
        side_specs.append(pl.BlockSpec((br, a.shape[1]),
                                       lambda i, j, nblk=nblk: (jnp.minimum(i * nc + j, nblk - 1), 0)))

    outs = pl.pallas_call(
        functools.partial(_nmm_pipelined_kernel, n_side=len(side_casts)),
        grid=(nt + 1, nc),
        in_specs=[pl.BlockSpec((ts, d), lambda i, j: (cur(i) * nc + j, 0)),
                  pl.BlockSpec((1, d), lambda i, j: (0, 0)),
                  pl.BlockSpec((None, 1, d), mod_map),
                  pl.BlockSpec((None, 1, d), mod_map),
                  pl.BlockSpec((d, tn), lambda i, j: (0, j))] + side_specs,
        out_specs=[pl.BlockSpec((tm, tn), lambda i, j: (jnp.maximum(i - 1, 0), jnp.where(i == 0, 0, j)))]
        + side_specs,
        out_shape=[jax.ShapeDtypeStruct((t, n), F32)] + [jax.ShapeDtypeStruct(a.shape, BF16) for a in side_casts],
        scratch_shapes=[pltpu.VMEM((tm, d), BF16), pltpu.VMEM((tm, d), BF16)],
        compiler_params=_cparams(("arbitrary", "arbitrary")),
        name=name,
    )(x2, g.reshape(1, d), shift.reshape(m, 1, d), scale.reshape(m, 1, d), w, *side_casts)
    return outs[0], tuple(outs[1:])


def _scan_pos(t, col_order, rows):
    if not col_order:
        return t
    cols = SUPER // rows
    return jnp.bitwise_and(t, cols - 1) * rows + jnp.right_shift(t, _ilog2(cols))


def _scan_constants(orders, rows):
    lc = _ilog2(CHUNK)
    half = CHUNK // 2
    dms, masks, rsels, cms = [], [], [], []
    for col_order in orders:
        pi = _scan_pos(lax.broadcasted_iota(jnp.int32, (SUPER, SUPER), 0), col_order, rows)
        pj = _scan_pos(lax.broadcasted_iota(jnp.int32, (SUPER, SUPER), 1), col_order, rows)
        same = jnp.right_shift(pi, lc) == jnp.right_shift(pj, lc)
        wj = jnp.bitwise_and(pj, CHUNK - 1)
        c8 = lax.broadcasted_iota(jnp.int32, (2 * N_SUB, SUPER), 0)
        p8 = _scan_pos(lax.broadcasted_iota(jnp.int32, (2 * N_SUB, SUPER), 1), col_order, rows)
        in_chunk = jnp.right_shift(p8, lc) == jnp.bitwise_and(c8, N_SUB - 1)
        w8 = jnp.bitwise_and(p8, CHUNK - 1)
        for fwd in (True, False):
            incl = same & ((pj <= pi) if fwd else (pj >= pi))
            upto_mid = same & ((wj < half) if fwd else (wj >= half))
            dms.append((incl.astype(F32) - upto_mid.astype(F32)).astype(BF16))
            masks.append(incl.astype(F32))
            sel = in_chunk & ((c8 >= N_SUB) | ((w8 < half) if fwd else (w8 >= half)))
            rsels.append(sel.astype(BF16))
        pr = jnp.right_shift(
            _scan_pos(lax.broadcasted_iota(jnp.int32, (SUPER, HG_HEAD_DIM), 0), col_order, rows), lc)
        cms.append(jnp.stack([(pr == c).astype(BF16) for c in range(N_SUB)]))
    return jnp.stack(dms), jnp.stack(masks), jnp.stack(rsels), jnp.stack(cms)


def _hgrn_kernel(q_ref, zf_ref, zb_ref, v_ref, g_ref, czf_ref, czb_ref, cv_ref, lbl_ref, gain_ref,
                 dm_ref, mask_ref, rsel_ref, cm_ref, o_ref, acc_ref, st_ref, *stage, col_order, rows):
    seq, width = q_ref.shape
    ctx_len = cv_ref.shape[0]
    heads = width // HG_HEAD_DIM
    n_lat = seq // SUPER
    n_ctx = ctx_len // SUPER
    n_slots = lbl_ref.shape[0] // 2
    cols = SUPER // rows
    d = HG_HEAD_DIM
    lat_slot = cm_ref.shape[0] - 1

    def lower_bound(direction):
        lg = lbl_ref[direction * n_slots:(direction + 1) * n_slots, :]
        e = jnp.exp(lg - jnp.max(lg, axis=0, keepdims=True))
        return e[0:1] / jnp.sum(e, axis=0, keepdims=True)

    lbs = (lower_bound(0), lower_bound(1))
    gain = gain_ref[...]

    def row_starts(sc, order):
        if not order:
            return [(pl.multiple_of(sc * SUPER, SUPER), SUPER)]
        return [(pl.multiple_of(r * GRID_W + sc * cols, cols), cols) for r in range(rows)]

    def load_rows(ref, sc, order):
        parts = [ref[pl.ds(s, n), :] for s, n in row_starts(sc, order)]
        return parts[0] if len(parts) == 1 else jnp.concatenate(parts, axis=0)

    def store_rows(ref, sc, order, val):
        off = 0
        for s, n in row_starts(sc, order):
            ref[pl.ds(s, n), :] = val[off:off + n]
            off += n

    def hsl(h):
        return slice(h * d, (h + 1) * d)

    def phase_a(sc, di, slot, order, q_r, z_r, v_r, bufs):
        qsub_ref, upd_ref, oin_ref, sums_ref = bufs
        k_idx = 2 * slot + di
        z = load_rows(z_r, sc, order)
        lb = lbs[di]
        f = lb + (1.0 - lb) * jax.nn.sigmoid(z)
        k = 1.0 - f
        lf_hi, lf_lo = _split_bf16(jnp.log(f))
        dm = dm_ref[k_idx]
        x1 = _dot(dm, lf_hi) + _dot(dm, lf_lo)
        rsel = rsel_ref[k_idx]
        sums_ref[...] = _dot(rsel, lf_hi) + _dot(rsel, lf_lo)
        ks = (k * jnp.exp(-x1)).astype(BF16)
        vb = load_rows(v_r, sc, order).astype(BF16)
        if q_r is not None:
            qs = (load_rows(q_r, sc, order) * jnp.exp(x1)).astype(BF16)
            allowed = mask_ref[k_idx] != 0.0
        for h in range(heads):
            sl = hsl(h)
            ks_h = ks[:, sl]
            vb_h = vb[:, sl]
            ks_sub = jnp.concatenate([ks_h * cm_ref[slot, c] for c in range(N_SUB)], axis=1)
            upd_ref[h] = _dot_tn(vb_h, ks_sub)
            if q_r is not None:
                qs_h = qs[:, sl]
                s = jnp.where(allowed, _dot_nt(qs_h, ks_h), 0.0).astype(BF16)
                oin_ref[:, sl] = _dot(s, vb_h)
                qsub_ref[:, h * N_SUB * d:(h + 1) * N_SUB * d] = jnp.concatenate(
                    [qs_h * cm_ref[slot, c] for c in range(N_SUB)], axis=1)

    def phase_b(sc, di, order, bufs, with_out):
        qsub_ref, upd_ref, oin_ref, sums_ref = bufs
        fwd = di == 0
        sums = sums_ref[...]
        a_mid = sums[0:N_SUB]
        a_end = sums[N_SUB:2 * N_SUB]
        upd_scale = jnp.exp(a_end - a_mid)
        decay = jnp.exp(a_end)
        mid_scale = jnp.exp(a_mid)
        outs = []
        for h in range(heads):
            sl = hsl(h)
            upd = upd_ref[h]
            st = st_ref[di * heads + h]
            entering = [None] * N_SUB
            for c in (range(N_SUB) if fwd else reversed(range(N_SUB))):
                entering[c] = st * mid_scale[c:c + 1, sl]
                st = st * decay[c:c + 1, sl] + upd[:, c * d:(c + 1) * d] * upd_scale[c:c + 1, sl]
            st_ref[di * heads + h] = st
            if with_out:
                st_cat = jnp.concatenate(entering, axis=1).astype(BF16)
                outs.append(oin_ref[:, sl] + _dot_nt(qsub_ref[:, h * N_SUB * d:(h + 1) * N_SUB * d], st_cat))
        if with_out:
            store_rows(acc_ref.at[di], sc, order, jnp.concatenate(outs, axis=1))

    def bufs(di, par):
        base = (2 * di + par) * 4
        return stage[base:base + 4]

    st_ref[...] = jnp.zeros_like(st_ref)

    def ctx_body(i, carry):
        for di, cz_r in ((0, czf_ref), (1, czb_ref)):
            sc = i if di == 0 else n_ctx - 1 - i
            phase_a(sc, di, 0, False, None, cz_r, cv_ref, bufs(di, 0))
            phase_b(sc, di, False, bufs(di, 0), False)
        return carry

    lax.fori_loop(0, n_ctx, ctx_body, 0)

    z_refs = (zf_ref, zb_ref)

    def lat_a(k, par):
        for di in range(2):
            phase_a(k if di == 0 else n_lat - 1 - k, di, lat_slot, col_order, q_ref, z_refs[di], v_ref,
                    bufs(di, par))

    def lat_b(k, par):
        for di in range(2):
            phase_b(k if di == 0 else n_lat - 1 - k, di, col_order, bufs(di, par), True)

    lat_a(0, 0)

    def pair_body(m, carry):
        k = 2 * m
        lat_a(k + 1, 1)
        lat_b(k, 0)
        lat_a(k + 2, 0)
        lat_b(k + 1, 1)
        return carry

    lax.fori_loop(0, n_lat // 2 - 1, pair_body, 0)
    lat_a(n_lat - 1, 1)
    lat_b(n_lat - 2, 0)
    lat_b(n_lat - 1, 1)

    def readout_body(i, carry):
        blk = pl.ds(pl.multiple_of(i * SUPER, SUPER), SUPER)
        o = acc_ref[0, blk, :] + acc_ref[1, blk, :]
        g = g_ref[blk, :]
        res = []
        for h in range(heads):
            oh = o[:, hsl(h)]
            res.append(oh * lax.rsqrt(jnp.mean(oh * oh, axis=-1, keepdims=True) + EPS))
        o_ref[blk, :] = (jnp.concatenate(res, axis=1) * gain * (g * jax.nn.sigmoid(g))).astype(o_ref.dtype)
        return carry

    lax.fori_loop(0, n_lat, readout_body, 0)


def _hgrn(p_hg, p_ctx, lb_logits, gain, head0, n_heads, col_order, rows, name):
    b, seq, _ = p_hg.shape
    ctx_len = p_ctx.shape[1]
    assert seq % SUPER == 0 and ctx_len % SUPER == 0 and 2 * N_SUB == SUBLANES
    assert not col_order or (rows * SUBLANES == SUPER and GRID_W % SUBLANES == 0)
    hps = 2
    width = hps * HG_HEAD_DIM
    groups = n_heads // hps
    part_blocks = HG_WIDTH // width
    g0 = head0 // hps
    n_rows = lb_logits.shape[0]
    consts = _scan_constants((False, True) if col_order else (False,), rows)

    def slab(length, part):
        return pl.BlockSpec((None, length, width), lambda bi, gi: (bi, 0, part * part_blocks + g0 + gi))

    whole = lambda a: pl.BlockSpec(a.shape, lambda bi, gi: (0,) * a.ndim)
    stage_bufs = [pltpu.VMEM((SUPER, hps * N_SUB * HG_HEAD_DIM), BF16),
                  pltpu.VMEM((hps, HG_HEAD_DIM, N_SUB * HG_HEAD_DIM), F32),
                  pltpu.VMEM((SUPER, width), F32),
                  pltpu.VMEM((2 * N_SUB, width), F32)]
    assert (seq // SUPER) % 2 == 0
    kern = functools.partial(_hgrn_kernel, col_order=col_order, rows=rows)
    return pl.pallas_call(
        kern,
        grid=(b, groups),
        in_specs=[slab(seq, 0), slab(seq, 1), slab(seq, 2), slab(seq, 3), slab(seq, 4),
                  slab(ctx_len, 0), slab(ctx_len, 1), slab(ctx_len, 2),
                  pl.BlockSpec((n_rows, width), lambda bi, gi: (0, g0 + gi)),
                  pl.BlockSpec((1, width), lambda bi, gi: (0, g0 + gi))] + [whole(a) for a in consts],
        out_specs=pl.BlockSpec((None, seq, width), lambda bi, gi: (bi, 0, gi)),
        out_shape=jax.ShapeDtypeStruct((b, seq, n_heads * HG_HEAD_DIM), BF16),
        scratch_shapes=[pltpu.VMEM((2, seq, width), F32),
                        pltpu.VMEM((2 * hps, HG_HEAD_DIM, HG_HEAD_DIM), F32)] + stage_bufs * 4,
        compiler_params=_cparams(("parallel", "parallel")),
        name=name,
    )(p_hg, p_hg, p_hg, p_hg, p_hg, p_ctx, p_ctx, p_ctx, lb_logits, gain, *consts)


def _dft_tables(seq):
    n = 2 * seq
    lo = GRID_W
    j = jnp.arange(n, dtype=jnp.int32)
    f = jnp.where(j < seq, j, j - seq)
    ang = lambda m: (m % n).astype(F32) * (2.0 * math.pi / n)
    a = ang(f[:, None] * (jnp.arange(seq // lo, dtype=jnp.int32) * lo)[None, :])
    b = ang(f[:, None] * jnp.arange(lo, dtype=jnp.int32)[None, :])
    ca, sa, cb, sb = jnp.cos(a), jnp.sin(a), jnp.cos(b), jnp.sin(b)
    t = jnp.arange(seq, dtype=jnp.int32)
    nyq = jnp.where(t % 2 == 0, 1.0, -1.0)

    cos_ft = (ca[:, :, None] * cb[:, None, :] - sa[:, :, None] * sb[:, None, :]).reshape(n, seq)
    sin_ft = (sa[:, :, None] * cb[:, None, :] + ca[:, :, None] * sb[:, None, :]).reshape(n, seq)
    jj = j[:, None]
    tab = jnp.where(jj < seq, cos_ft, jnp.where(jj == seq, nyq[None, :], sin_ft)).astype(BF16)
    tab = lax.optimization_barrier(tab)
    return tab, tab.T


def _filter_embedding(seq):
    pos = jnp.arange(seq, dtype=F32)[:, None]
    t = pos / max(seq - 1, 1)
    bands = jnp.linspace(1e-4, N_BANDS - 1, N_BANDS, dtype=F32)[None, :]
    ang = bands * (2.0 * math.pi) * pos / seq
    return jnp.concatenate([t, jnp.cos(ang), -jnp.sin(ang)], axis=-1)


def _filter_deltas(width):
    return jnp.abs(jnp.linspace(math.log(FILTER_TARGET) / SLOW_DECAY_PCT,
                                math.log(FILTER_TARGET) / FAST_DECAY_PCT, width, dtype=F32))


def _kspec_kernel(z_ref, w1_ref, b1_ref, fr_ref, w2_ref, b2_ref, w3f_ref, w3b_ref, dl_ref, f_ref, k_ref, h_ref):
    seq = z_ref.shape[0]
    ct = k_ref.shape[1]

    @pl.when(pl.program_id(0) == 0)
    def _():
        fr = fr_ref[...]
        h1 = jnp.sin(fr * (_dot3(z_ref[...], w1_ref[...]) + b1_ref[...]))
        h_ref[...] = jnp.sin(fr * (_dot3(h1, w2_ref[...]) + b2_ref[...]))

    h = h_ref[...]
    row = lax.broadcasted_iota(jnp.int32, (seq, ct), 0)
    t = row.astype(F32) / max(seq - 1, 1)
    window = jnp.exp(-t * dl_ref[...]) + FILTER_SHIFT
    hf = _dot3(h, w3f_ref[...]) * window
    hb = jnp.where(row == 0, 0.0, _dot3(h, w3b_ref[...]) * window)
    r = lax.rsqrt(jnp.sum(hf * hf, axis=0, keepdims=True) + jnp.sum(hb * hb, axis=0, keepdims=True))
    plus = ((hf + hb) * r).astype(BF16)
    minus = ((hf - hb) * r).astype(BF16)
    k_ref[0:seq, :] = _dot(f_ref[0:seq, :], plus)
    k_ref[seq:2 * seq, :] = _dot(f_ref[seq:2 * seq, :], minus)
    nyq = _dot(f_ref[seq:seq + SUBLANES, :], plus)
    first = lax.broadcasted_iota(jnp.int32, (SUBLANES, ct), 0) == 0
    k_ref[seq:seq + SUBLANES, :] = jnp.where(first, nyq, k_ref[seq:seq + SUBLANES, :])


def _kspec(z, w1, b1, freq, w2, b2, w3, deltas, fmat, ct):
    seq = z.shape[0]
    ch = deltas.shape[1]
    hid = w2.shape[0]
    nb = ch // ct
    full = lambda a: pl.BlockSpec(a.shape, lambda c: (0,) * a.ndim)
    return pl.pallas_call(
        _kspec_kernel,
        grid=(nb,),
        in_specs=[full(z), full(w1), full(b1), full(freq), full(w2), full(b2),
                  pl.BlockSpec((hid, ct), lambda c: (0, c)),
                  pl.BlockSpec((hid, ct), lambda c: (0, nb + c)),
                  pl.BlockSpec((1, ct), lambda c: (0, c)),
                  pl.BlockSpec(fmat.shape, lambda c: (0, 0), pipeline_mode=pl.Buffered(1))],
        out_specs=pl.BlockSpec((2 * seq, ct), lambda c: (0, c)),
        out_shape=jax.ShapeDtypeStruct((2 * seq, ch), F32),
        scratch_shapes=[pltpu.VMEM((seq, hid), F32)],
        compiler_params=_cparams(("arbitrary",)),
        name="hyena_filter_spectrum",
    )(z, w1, b1, freq, w2, b2, w3, w3, deltas, fmat)


def _short_conv(t, w, b):
    seq = t.shape[0]
    row = lax.broadcasted_iota(jnp.int32, t.shape, 0)
    prev = jnp.where(row == 0, 0.0, pltpu.roll(t, 1, 0))
    nxt = jnp.where(row == seq - 1, 0.0, pltpu.roll(t, seq - 1, 0))
    return prev * w[0:1] + t * w[1:2] + nxt * w[2:3] + b


def _hy_fwd_kernel(v_ref, x1_ref, wv_ref, wx1_ref, bv_ref, bx1_ref, f_ref, k_ref, z_ref, u_ref):
    seq = v_ref.shape[0]
    n = 2 * seq
    vc = _short_conv(v_ref[...], wv_ref[...], bv_ref[...])
    x1c = _short_conv(x1_ref[...], wx1_ref[...], bx1_ref[...])
    ub = (x1c * vc).astype(BF16)
    u_ref[...] = ub
    fs = seq // HY_SPLIT
    for r in range(HY_SPLIT):
        xa = _dot(f_ref[r * fs:(r + 1) * fs, :], ub)
        xb = _dot(f_ref[seq + r * fs:seq + (r + 1) * fs, :], ub)
        ka = k_ref[r * fs:(r + 1) * fs, :]
        kb = k_ref[seq + r * fs:seq + (r + 1) * fs, :]
        bb = xb * kb
        if r == 0:
            first = lax.broadcasted_iota(jnp.int32, xa.shape, 0) == 0
            w = jnp.where(first, 1.0 / n, 2.0 / n)
            za = (xa * ka - jnp.where(first, 0.0, bb)) * w
            zb = jnp.where(first, bb, xa * kb + xb * ka) * w
        else:
            za = (xa * ka - bb) * (2.0 / n)
            zb = (xa * kb + xb * ka) * (2.0 / n)
        z_ref[r * fs:(r + 1) * fs, :] = za.astype(z_ref.dtype)
        z_ref[seq + r * fs:seq + (r + 1) * fs, :] = zb.astype(z_ref.dtype)


def _hy_inv_kernel(z_ref, g_ref, u_ref, x0_ref, wx0_ref, bx0_ref, bias_ref, o_ref):
    seq = x0_ref.shape[0]
    x0c = _short_conv(x0_ref[...], wx0_ref[...], bx0_ref[...])
    ts = seq // HY_SPLIT
    for r in range(HY_SPLIT):
        rs = slice(r * ts, (r + 1) * ts)
        y = _dot(g_ref[rs, :], z_ref[...])
        o_ref[rs, :] = (x0c[rs] * (y + u_ref[rs, :].astype(F32) * bias_ref[...])).astype(o_ref.dtype)


def _hyena(p_hy, col0, conv_w, conv_b, kspec, bias, fmat, gmat, ct):
    b, seq, _ = p_hy.shape
    ch = bias.shape[1]
    nb = ch // ct
    taps = conv_w.shape[0]
    assert col0 % ct == 0
    act = lambda part: pl.BlockSpec((None, seq, ct), lambda c, bi: (bi, 0, col0 // ct + part * nb + c))
    cw = lambda part: pl.BlockSpec((taps, ct), lambda c, bi: (0, part * nb + c))
    cb = lambda part: pl.BlockSpec((1, ct), lambda c, bi: (0, part * nb + c))
    const = lambda a: pl.BlockSpec(a.shape, lambda c, bi: (0, 0), pipeline_mode=pl.Buffered(1))
    zspec = pl.BlockSpec((None, 2 * seq, ct), lambda c, bi: (bi, 0, c))
    uspec = pl.BlockSpec((None, seq, ct), lambda c, bi: (bi, 0, c))
    z, u = pl.pallas_call(
        _hy_fwd_kernel,
        grid=(nb, b),
        in_specs=[act(0), act(1), cw(0), cw(1), cb(0), cb(1), const(fmat),
                  pl.BlockSpec((2 * seq, ct), lambda c, bi: (0, c))],
        out_specs=[zspec, uspec],
        out_shape=[jax.ShapeDtypeStruct((b, 2 * seq, ch), BF16), jax.ShapeDtypeStruct((b, seq, ch), BF16)],
        compiler_params=_cparams(("parallel", "parallel")),
        name="hyena_forward_dft",
    )(p_hy, p_hy, conv_w, conv_w, conv_b, conv_b, fmat, kspec)
    return pl.pallas_call(
        _hy_inv_kernel,
        grid=(nb, b),
        in_specs=[zspec, const(gmat), uspec, act(2), cw(2), cb(2), pl.BlockSpec((1, ct), lambda c, bi: (0, c))],
        out_specs=pl.BlockSpec((None, seq, ct), lambda c, bi: (bi, 0, c)),
        out_shape=jax.ShapeDtypeStruct((b, seq, ch), BF16),
        compiler_params=_cparams(("parallel", "parallel")),
        name="hyena_inverse_dft",
    )(z, gmat, u, p_hy, conv_w, conv_b, bias)


def _wout_kernel(ar_ref, ac_ref, y_ref, x_ref, gate_ref, w_ref, o_ref):
    mix = jnp.concatenate([ar_ref[...], ac_ref[...], y_ref[...]], axis=1)
    o_ref[...] = x_ref[...] + gate_ref[...] * _dot(mix, w_ref[...])


def _wout(a_row, a_col, y, x2, gate, w, rows_per_mod, tm):
    t, d = x2.shape
    m = gate.shape[0]
    per = rows_per_mod // tm
    wr, wc, wy = a_row.shape[1], a_col.shape[1], y.shape[1]
    return pl.pallas_call(
        _wout_kernel,
        grid=(t // tm,),
        in_specs=[pl.BlockSpec((tm, wr), lambda i: (i, 0)),
                  pl.BlockSpec((tm, wc), lambda i: (i, 0)),
                  pl.BlockSpec((tm, wy), lambda i: (i, 0)),
                  pl.BlockSpec((tm, d), lambda i: (i, 0)),
                  pl.BlockSpec((None, 1, d), lambda i: (i // per, 0, 0)),
                  pl.BlockSpec(w.shape, lambda i: (0, 0))],
        out_specs=pl.BlockSpec((tm, d), lambda i: (i, 0)),
        out_shape=jax.ShapeDtypeStruct((t, d), F32),
        compiler_params=_cparams(("parallel",)),
        name="out_projection_residual",
    )(a_row, a_col, y, x2, gate.reshape(m, 1, d), w)


def _mlp_kernel(x_ref, g_ref, sh_ref, sc_ref, gate_ref, w1_ref, w2_ref, fg_ref, o_ref, u_ref, acc_ref):
    j = pl.program_id(1)
    last = pl.num_programs(1) - 1
    sub = u_ref.shape[0] // MLP_ROW_SPLIT

    def step(first, final):
        for r in range(MLP_ROW_SPLIT):
            rs = slice(r * sub, (r + 1) * sub)
            if first:
                y = _rms(x_ref[rs, :], g_ref[...])
                u_ref[rs, :] = (y * (1.0 + sc_ref[...]) + sh_ref[...]).astype(BF16)
            h = jnp.maximum(_dot(u_ref[rs, :], w1_ref[...]), 0.0)
            acc = _dot((h * h).astype(BF16), w2_ref[...])
            if not first:
                acc = acc_ref[rs, :] + acc
            if final:
                o_ref[rs, :] = _rms(x_ref[rs, :] + gate_ref[...] * acc, fg_ref[...])
            else:
                acc_ref[rs, :] = acc

    pl.when(j == 0)(functools.partial(step, True, False))
    pl.when((j > 0) & (j < last))(functools.partial(step, False, False))
    pl.when(j == last)(functools.partial(step, False, True))


def _mlp(x2, g, shift, scale, gate, w1, w2, fg, rows_per_mod, tm, tf):
    t, d = x2.shape
    dff = w1.shape[1]
    m = shift.shape[0]
    per = rows_per_mod // tm
    mod = pl.BlockSpec((None, 1, d), lambda i, j: (i // per, 0, 0))
    vec = pl.BlockSpec((1, d), lambda i, j: (0, 0))
    r3 = lambda a: a.reshape(m, 1, d)
    return pl.pallas_call(
        _mlp_kernel,
        grid=(t // tm, dff // tf),
        in_specs=[pl.BlockSpec((tm, d), lambda i, j: (i, 0)), vec, mod, mod, mod,
                  pl.BlockSpec((d, tf), lambda i, j: (0, j)),
                  pl.BlockSpec((tf, d), lambda i, j: (j, 0)),
                  vec],
        out_specs=pl.BlockSpec((tm, d), lambda i, j: (i, 0)),
        out_shape=jax.ShapeDtypeStruct((t, d), F32),
        scratch_shapes=[pltpu.VMEM((tm, d), BF16), pltpu.VMEM((tm, d), F32)],
        compiler_params=_cparams(("parallel", "arbitrary")),
        name="mlp_residual_final_norm",
    )(x2, g.reshape(1, d), r3(shift), r3(scale), r3(gate), w1, w2, fg.reshape(1, d))


def kernel(x, c, ctx, c_ctx, w_ada, b_ada, norm1_g, w_in, hgrn_lb_logits, hgrn_norm_g, hy_conv_w, hy_conv_b,
           flt_w1, flt_b1, flt_freq, flt_w2, flt_b2, flt_w3, hy_bias, w_out, norm2_g, w_mlp1, w_mlp2,
           final_norm_g):
    b, seq, d = x.shape
    ctx_len = ctx.shape[1]
    depth = w_ada.shape[0]
    assert depth == 1, "single-layer block"
    rows = seq // GRID_W
    hg_cols = 5 * HG_WIDTH
    hy_width = d - HG_WIDTH
    layer = 0

    pad = (-(b + 1)) % SUBLANES
    stacked = jnp.concatenate([c, c_ctx[None, :], jnp.zeros((pad, d), F32)], axis=0)
    mod = _ada(stacked, w_ada[layer], b_ada[layer])
    sh1, sc1, g1, sh2, sc2, g2 = [mod[:b, i * d:(i + 1) * d] for i in range(N_MOD)]
    csh1, csc1 = mod[b:b + 1, 0:d], mod[b:b + 1, d:2 * d]

    w_in_b = w_in[layer].astype(BF16)
    x2 = x.reshape(b * seq, d)
    ctx2 = ctx.reshape(b * ctx_len, d)
    p_lat, (w_out_b, w_mlp1_b, w_mlp2_b) = _norm_mod_matmul_pipelined(
        x2, norm1_g[layer], sh1, sc1, w_in_b, seq, 1024, 2048, "in_projection",
        side_casts=(w_out[layer], w_mlp1[layer], w_mlp2[layer]))
    p_lat = p_lat.reshape(b, seq, -1)
    p_ctx = _norm_mod_matmul(ctx2, norm1_g[layer], csh1, csc1, w_in_b[:, HG_WIDTH:4 * HG_WIDTH], b * ctx_len,
                             F32, min(1024, b * ctx_len), 1024, "in_projection_context")
    p_ctx = p_ctx.reshape(b, ctx_len, 3 * HG_WIDTH)

    lbl = hgrn_lb_logits.astype(F32).reshape(2 * (depth + 1), HG_WIDTH)
    gain = hgrn_norm_g[layer].reshape(1, HG_WIDTH)
    a_row = _hgrn(p_lat, p_ctx, lbl, gain, 0, HG_ROW_HEADS, False, rows, "hgrn_row_heads")
    a_col = _hgrn(p_lat, p_ctx, lbl, gain, HG_ROW_HEADS, HG_HEADS - HG_ROW_HEADS, True, rows, "hgrn_col_heads")

    fmat, gmat = _dft_tables(seq)
    emb = _filter_embedding(seq)
    kpad = (-emb.shape[1]) % LANES
    emb = jnp.pad(emb, ((0, 0), (0, kpad)))
    fw1 = jnp.pad(flt_w1[layer], ((0, kpad), (0, 0)))
    row = lambda a: a[layer].reshape(1, -1)
    kspec = _kspec(emb, fw1, row(flt_b1), row(flt_freq), flt_w2[layer], row(flt_b2), flt_w3[layer],
                   _filter_deltas(hy_width)[None, :], fmat, 256)
    y_lat = _hyena(p_lat, hg_cols, hy_conv_w[layer], hy_conv_b[layer].reshape(1, -1), kspec, row(hy_bias),
                   fmat, gmat, 256)

    x_mid = _wout(a_row.reshape(b * seq, -1), a_col.reshape(b * seq, -1), y_lat.reshape(b * seq, hy_width),
                  x2, g1, w_out_b, seq, 512)
    out = _mlp(x_mid, norm2_g[layer], sh2, sc2, g2, w_mlp1_b, w_mlp2_b, final_norm_g, seq, 512, 1024)
    return out.reshape(b, seq, d)
```

```python
import functools
import math

import jax
import jax.numpy as jnp
from jax import lax
from jax.experimental import pallas as pl
from jax.experimental.pallas import tpu as pltpu

GRID_W = 64
HG_WIDTH = 1024
HG_HEAD_DIM = 128
HG_HEADS = HG_WIDTH // HG_HEAD_DIM
HG_ROW_HEADS = HG_HEADS // 2
CHUNK = 64
N_BANDS = 16
FILTER_TARGET = 1e-2
FAST_DECAY_PCT = 0.3
SLOW_DECAY_PCT = 1.5
FILTER_SHIFT = 0.05
N_MOD = 6
EPS = 1e-6

F32 = jnp.float32
BF16 = jnp.bfloat16

V7X_VMEM_BYTES = 64 * 1024 * 1024
VMEM_LIMIT = 56 * 1024 * 1024
LANES = 128
SUBLANES = 8

SUPER = 256
N_SUB = SUPER // CHUNK

MLP_ROW_SPLIT = 2
HY_SPLIT = 2


def _cparams(sem, flags=None):
    return pltpu.CompilerParams(dimension_semantics=sem, vmem_limit_bytes=VMEM_LIMIT, flags=flags)


def _split_bf16(a):
    hi = a.astype(BF16)
    lo = (a - hi.astype(F32)).astype(BF16)
    return hi, lo


def _dot(a, b):
    return jnp.dot(a, b, preferred_element_type=F32)


def _dot_nt(a, b):
    return lax.dot_general(a, b, (((1,), (1,)), ((), ())), preferred_element_type=F32)


def _dot_tn(a, b):
    return lax.dot_general(a, b, (((0,), (0,)), ((), ())), preferred_element_type=F32)


def _dot3(a, b):
    ah, al = _split_bf16(a)
    bh, bl = _split_bf16(b)
    return _dot(ah, bh) + _dot(ah, bl) + _dot(al, bh)


def _rms(x, g):
    return x * lax.rsqrt(jnp.mean(x * x, axis=-1, keepdims=True) + EPS) * g


def _ilog2(n):
    assert n > 0 and n & (n - 1) == 0, n
    return n.bit_length() - 1


def _ada_kernel(s_ref, w_ref, b_ref, o_ref):
    s = s_ref[...]
    s = s * jax.nn.sigmoid(s)
    o_ref[...] = _dot3(s, w_ref[...]) + b_ref[...]


def _ada(stacked, w, b):
    r, d = stacked.shape
    n = w.shape[1]
    tn = 512
    return pl.pallas_call(
        _ada_kernel,
        grid=(n // tn,),
        in_specs=[pl.BlockSpec((r, d), lambda j: (0, 0)),
                  pl.BlockSpec((d, tn), lambda j: (0, j)),
                  pl.BlockSpec((1, tn), lambda j: (0, j))],
        out_specs=pl.BlockSpec((r, tn), lambda j: (0, j)),
        out_shape=jax.ShapeDtypeStruct((r, n), F32),
        compiler_params=_cparams(("parallel",)),
        name="ada_modulation",
    )(stacked, w, b.reshape(1, n))


def _nmm_kernel(x_ref, g_ref, sh_ref, sc_ref, w_ref, o_ref, u_ref):
    @pl.when(pl.program_id(1) == 0)
    def _():
        y = _rms(x_ref[...], g_ref[...])
        u_ref[...] = (y * (1.0 + sc_ref[...]) + sh_ref[...]).astype(BF16)

    o_ref[...] = _dot(u_ref[...], w_ref[...]).astype(o_ref.dtype)


def _norm_mod_matmul(x2, g, shift, scale, w, rows_per_mod, out_dtype, tm, tn, name):
    t, d = x2.shape
    n = w.shape[1]
    m = shift.shape[0]
    per = rows_per_mod // tm
    mod_map = lambda i, j: (i // per, 0, 0)
    return pl.pallas_call(
        _nmm_kernel,
        grid=(t // tm, n // tn),
        in_specs=[pl.BlockSpec((tm, d), lambda i, j: (i, 0)),
                  pl.BlockSpec((1, d), lambda i, j: (0, 0)),
                  pl.BlockSpec((None, 1, d), mod_map),
                  pl.BlockSpec((None, 1, d), mod_map),
                  pl.BlockSpec((d, tn), lambda i, j: (0, j))],
        out_specs=pl.BlockSpec((tm, tn), lambda i, j: (i, j)),
        out_shape=jax.ShapeDtypeStruct((t, n), out_dtype),
        scratch_shapes=[pltpu.VMEM((tm, d), BF16)],
        compiler_params=_cparams(("parallel", "arbitrary")),
        name=name,
    )(x2, g.reshape(1, d), shift.reshape(m, 1, d), scale.reshape(m, 1, d), w)


def _nmm_pipelined_kernel(*refs, n_side):
    xs_ref, g_ref, sh_ref, sc_ref, w_ref = refs[:5]
    side_in = refs[5:5 + n_side]
    o_ref = refs[5 + n_side]
    side_out = refs[6 + n_side:6 + 2 * n_side]
    ua_ref, ub_ref = refs[6 + 2 * n_side:]
    i = pl.program_id(0)
    j = pl.program_id(1)
    ts = xs_ref.shape[0]

    @pl.when((i == 0) & (j == 0))
    def _():
        ub_ref[...] = jnp.zeros_like(ub_ref)

    def body(fill_ref, use_ref):
        y = _rms(xs_ref[...], g_ref[...])
        fill_ref[pl.ds(pl.multiple_of(j * ts, ts), ts), :] = (y * (1.0 + sc_ref[...]) + sh_ref[...]).astype(BF16)
        for src, dst in zip(side_in, side_out):
            dst[...] = src[...].astype(BF16)
        o_ref[...] = _dot(use_ref[...], w_ref[...])

    pl.when(i % 2 == 0)(functools.partial(body, ua_ref, ub_ref))
    pl.when(i % 2 == 1)(functools.partial(body, ub_ref, ua_ref))


def _norm_mod_matmul_pipelined(x2, g, shift, scale, w, rows_per_mod, tm, tn, name, side_casts=()):
    t, d = x2.shape
    n = w.shape[1]
    m = shift.shape[0]
    per = rows_per_mod // tm
    nt, nc = t // tm, n // tn
    ts = tm // nc
    assert ts * nc == tm and ts % SUBLANES == 0
    steps = (nt + 1) * nc
    cur = lambda i: jnp.minimum(i, nt - 1)
    mod_map = lambda i, j: (cur(i) // per, 0, 0)

    side_specs = []
    for a in side_casts:
        rows = a.shape[0]
        br = next(r for r in range(2 * SUBLANES, rows + 1, 2 * SUBLANES) if rows % r == 0 and rows // r <= steps)
        nblk = rows // br
        side_specs.append(pl.BlockSpec((br, a.shape[1]),
                                       lambda i, j, nblk=nblk: (jnp.minimum(i * nc + j, nblk - 1), 0)))

    outs = pl.pallas_call(
        functools.partial(_nmm_pipelined_kernel, n_side=len(side_casts)),
        grid=(nt + 1, nc),
        in_specs=[pl.BlockSpec((ts, d), lambda i, j: (cur(i) * nc + j, 0)),
                  pl.BlockSpec((1, d), lambda i, j: (0, 0)),
                  pl.BlockSpec((None, 1, d), mod_map),
                  pl.BlockSpec((None, 1, d), mod_map),
                  pl.BlockSpec((d, tn), lambda i, j: (0, j))] + side_specs,
        out_specs=[pl.BlockSpec((tm, tn), lambda i, j: (jnp.maximum(i - 1, 0), jnp.where(i == 0, 0, j)))]
        + side_specs,
        out_shape=[jax.ShapeDtypeStruct((t, n), F32)] + [jax.ShapeDtypeStruct(a.shape, BF16) for a in side_casts],
        scratch_shapes=[pltpu.VMEM((tm, d), BF16), pltpu.VMEM((tm, d), BF16)],
        compiler_params=_cparams(("arbitrary", "arbitrary")),
        name=name,
    )(x2, g.reshape(1, d), shift.reshape(m, 1, d), scale.reshape(m, 1, d), w, *side_casts)
    return outs[0], tuple(outs[1:])


def _scan_pos(t, col_order, rows):
    if not col_order:
        return t
    cols = SUPER // rows
    return jnp.bitwise_and(t, cols - 1) * rows + jnp.right_shift(t, _ilog2(cols))


def _scan_constants(orders, rows):
    lc = _ilog2(CHUNK)
    half = CHUNK // 2
    dms, masks, rsels, cms = [], [], [], []
    for col_order in orders:
        pi = _scan_pos(lax.broadcasted_iota(jnp.int32, (SUPER, SUPER), 0), col_order, rows)
        pj = _scan_pos(lax.broadcasted_iota(jnp.int32, (SUPER, SUPER), 1), col_order, rows)
        same = jnp.right_shift(pi, lc) == jnp.right_shift(pj, lc)
        wj = jnp.bitwise_and(pj, CHUNK - 1)
        c8 = lax.broadcasted_iota(jnp.int32, (2 * N_SUB, SUPER), 0)
        p8 = _scan_pos(lax.broadcasted_iota(jnp.int32, (2 * N_SUB, SUPER), 1), col_order, rows)
        in_chunk = jnp.right_shift(p8, lc) == jnp.bitwise_and(c8, N_SUB - 1)
        w8 = jnp.bitwise_and(p8, CHUNK - 1)
        for fwd in (True, False):
            incl = same & ((pj <= pi) if fwd else (pj >= pi))
            upto_mid = same & ((wj < half) if fwd else (wj >= half))
            dms.append((incl.astype(F32) - upto_mid.astype(F32)).astype(BF16))
            masks.append(incl.astype(F32))
            sel = in_chunk & ((c8 >= N_SUB) | ((w8 < half) if fwd else (w8 >= half)))
            rsels.append(sel.astype(BF16))
        pr = jnp.right_shift(
            _scan_pos(lax.broadcasted_iota(jnp.int32, (SUPER, HG_HEAD_DIM), 0), col_order, rows), lc)
        cms.append(jnp.stack([(pr == c).astype(BF16) for c in range(N_SUB)]))
    return jnp.stack(dms), jnp.stack(masks), jnp.stack(rsels), jnp.stack(cms)


def _hgrn_kernel(q_ref, zf_ref, zb_ref, v_ref, g_ref, czf_ref, czb_ref, cv_ref, lbl_ref, gain_ref,
                 dm_ref, mask_ref, rsel_ref, cm_ref, o_ref, acc_ref, st_ref, *stage, col_order, rows):
    seq, width = q_ref.shape
    ctx_len = cv_ref.shape[0]
    heads = width // HG_HEAD_DIM
    n_lat = seq // SUPER
    n_ctx = ctx_len // SUPER
    n_slots = lbl_ref.shape[0] // 2
    cols = SUPER // rows
    d = HG_HEAD_DIM
    lat_slot = cm_ref.shape[0] - 1

    def lower_bound(direction):
        lg = lbl_ref[direction * n_slots:(direction + 1) * n_slots, :]
        e = jnp.exp(lg - jnp.max(lg, axis=0, keepdims=True))
        return e[0:1] / jnp.sum(e, axis=0, keepdims=True)

    lbs = (lower_bound(0), lower_bound(1))
    gain = gain_ref[...]

    def row_starts(sc, order):
        if not order:
            return [(sc * SUPER, SUPER)]
        return [(r * GRID_W + sc * cols, cols) for r in range(rows)]

    def load_rows(ref, sc, order):
        parts = [ref[pl.ds(s, n), :] for s, n in row_starts(sc, order)]
        return parts[0] if len(parts) == 1 else jnp.concatenate(parts, axis=0)

    def store_rows(ref, sc, order, val):
        off = 0
        for s, n in row_starts(sc, order):
            ref[pl.ds(s, n), :] = val[off:off + n]
            off += n

    def hsl(h):
        return slice(h * d, (h + 1) * d)

    def phase_a(sc, di, slot, order, q_r, z_r, v_r, bufs):
        qsub_ref, upd_ref, oin_ref, sums_ref = bufs
        k_idx = 2 * slot + di
        z = load_rows(z_r, sc, order)
        lb = lbs[di]
        f = lb + (1.0 - lb) * jax.nn.sigmoid(z)
        k = 1.0 - f
        lf_hi, lf_lo = _split_bf16(jnp.log(f))
        dm = dm_ref[k_idx]
        x1 = _dot(dm, lf_hi) + _dot(dm, lf_lo)
        rsel = rsel_ref[k_idx]
        sums_ref[...] = _dot(rsel, lf_hi) + _dot(rsel, lf_lo)
        ks = (k * jnp.exp(-x1)).astype(BF16)
        vb = load_rows(v_r, sc, order).astype(BF16)
        if q_r is not None:
            qs = (load_rows(q_r, sc, order) * jnp.exp(x1)).astype(BF16)
            allowed = mask_ref[k_idx] != 0.0
        for h in range(heads):
            sl = hsl(h)
            ks_h = ks[:, sl]
            vb_h = vb[:, sl]
            ks_sub = jnp.concatenate([ks_h * cm_ref[slot, c] for c in range(N_SUB)], axis=1)
            upd_ref[h] = _dot_tn(vb_h, ks_sub)
            if q_r is not None:
                qs_h = qs[:, sl]
                s = jnp.where(allowed, _dot_nt(qs_h, ks_h), 0.0).astype(BF16)
                oin_ref[:, sl] = _dot(s, vb_h)
                qsub_ref[:, h * N_SUB * d:(h + 1) * N_SUB * d] = jnp.concatenate(
                    [qs_h * cm_ref[slot, c] for c in range(N_SUB)], axis=1)

    def phase_b(sc, di, order, bufs, with_out):
        qsub_ref, upd_ref, oin_ref, sums_ref = bufs
        fwd = di == 0
        sums = sums_ref[...]
        a_mid = sums[0:N_SUB]
        a_end = sums[N_SUB:2 * N_SUB]
        upd_scale = jnp.exp(a_end - a_mid)
        decay = jnp.exp(a_end)
        mid_scale = jnp.exp(a_mid)
        outs = []
        for h in range(heads):
            sl = hsl(h)
            upd = upd_ref[h]
            st = st_ref[di * heads + h]
            entering = [None] * N_SUB
            for c in (range(N_SUB) if fwd else reversed(range(N_SUB))):
                entering[c] = st * mid_scale[c:c + 1, sl]
                st = st * decay[c:c + 1, sl] + upd[:, c * d:(c + 1) * d] * upd_scale[c:c + 1, sl]
            st_ref[di * heads + h] = st
            if with_out:
                st_cat = jnp.concatenate(entering, axis=1).astype(BF16)
                outs.append(oin_ref[:, sl] + _dot_nt(qsub_ref[:, h * N_SUB * d:(h + 1) * N_SUB * d], st_cat))
        if with_out:
            store_rows(acc_ref.at[di], sc, order, jnp.concatenate(outs, axis=1))

    def bufs(di, par):
        base = (2 * di + par) * 4
        return stage[base:base + 4]

    def readout(sc):
        o = load_rows(acc_ref.at[0], sc, col_order) + load_rows(acc_ref.at[1], sc, col_order)
        g = load_rows(g_ref, sc, col_order)
        res = []
        for h in range(heads):
            oh = o[:, hsl(h)]
            res.append(oh * lax.rsqrt(jnp.mean(oh * oh, axis=-1, keepdims=True) + EPS))
        store_rows(acc_ref.at[0], sc, col_order, jnp.concatenate(res, axis=1) * gain * (g * jax.nn.sigmoid(g)))

    st_ref[...] = jnp.zeros_like(st_ref)
    for i in range(n_ctx):
        for di, cz_r in ((0, czf_ref), (1, czb_ref)):
            sc = i if di == 0 else n_ctx - 1 - i
            phase_a(sc, di, 0, False, None, cz_r, cv_ref, bufs(di, 0))
            phase_b(sc, di, False, bufs(di, 0), False)

    z_refs = (zf_ref, zb_ref)
    for k in range(n_lat + 1):
        if k < n_lat:
            for di in range(2):
                phase_a(k if di == 0 else n_lat - 1 - k, di, lat_slot, col_order, q_ref, z_refs[di], v_ref,
                        bufs(di, k % 2))
        if k >= 1:
            for di in range(2):
                phase_b(k - 1 if di == 0 else n_lat - k, di, col_order, bufs(di, (k - 1) % 2), True)
            if 2 * (k - 1) >= n_lat - 1:
                for sc in sorted({k - 1, n_lat - k}):
                    readout(sc)
    o_ref[...] = acc_ref[0].astype(o_ref.dtype)


def _hgrn(p_hg, p_ctx, lb_logits, gain, head0, n_heads, col_order, rows, name):
    b, seq, _ = p_hg.shape
    ctx_len = p_ctx.shape[1]
    assert seq % SUPER == 0 and ctx_len % SUPER == 0 and 2 * N_SUB == SUBLANES
    assert not col_order or (rows * SUBLANES == SUPER and GRID_W % SUBLANES == 0)
    hps = 2
    width = hps * HG_HEAD_DIM
    groups = n_heads // hps
    part_blocks = HG_WIDTH // width
    g0 = head0 // hps
    n_rows = lb_logits.shape[0]
    consts = _scan_constants((False, True) if col_order else (False,), rows)

    def slab(length, part):
        return pl.BlockSpec((None, length, width), lambda bi, gi: (bi, 0, part * part_blocks + g0 + gi))

    whole = lambda a: pl.BlockSpec(a.shape, lambda bi, gi: (0,) * a.ndim)
    stage_bufs = [pltpu.VMEM((SUPER, hps * N_SUB * HG_HEAD_DIM), BF16),
                  pltpu.VMEM((hps, HG_HEAD_DIM, N_SUB * HG_HEAD_DIM), F32),
                  pltpu.VMEM((SUPER, width), F32),
                  pltpu.VMEM((2 * N_SUB, width), F32)]
    assert (seq // SUPER) % 2 == 0
    kern = functools.partial(_hgrn_kernel, col_order=col_order, rows=rows)
    return pl.pallas_call(
        kern,
        grid=(b, groups),
        in_specs=[slab(seq, 0), slab(seq, 1), slab(seq, 2), slab(seq, 3), slab(seq, 4),
                  slab(ctx_len, 0), slab(ctx_len, 1), slab(ctx_len, 2),
                  pl.BlockSpec((n_rows, width), lambda bi, gi: (0, g0 + gi)),
                  pl.BlockSpec((1, width), lambda bi, gi: (0, g0 + gi))] + [whole(a) for a in consts],
        out_specs=pl.BlockSpec((None, seq, width), lambda bi, gi: (bi, 0, gi)),
        out_shape=jax.ShapeDtypeStruct((b, seq, n_heads * HG_HEAD_DIM), BF16),
        scratch_shapes=[pltpu.VMEM((2, seq, width), F32),
                        pltpu.VMEM((2 * hps, HG_HEAD_DIM, HG_HEAD_DIM), F32)] + stage_bufs * 4,
        compiler_params=_cparams(("parallel", "parallel")),
        name=name,
    )(p_hg, p_hg, p_hg, p_hg, p_hg, p_ctx, p_ctx, p_ctx, lb_logits, gain, *consts)


def _dft_tables(seq):
    n = 2 * seq
    lo = GRID_W
    j = jnp.arange(n, dtype=jnp.int32)
    f = jnp.where(j < seq, j, j - seq)
    ang = lambda m: (m % n).astype(F32) * (2.0 * math.pi / n)
    a = ang(f[:, None] * (jnp.arange(seq // lo, dtype=jnp.int32) * lo)[None, :])
    b = ang(f[:, None] * jnp.arange(lo, dtype=jnp.int32)[None, :])
    ca, sa, cb, sb = jnp.cos(a), jnp.sin(a), jnp.cos(b), jnp.sin(b)
    t = jnp.arange(seq, dtype=jnp.int32)
    nyq = jnp.where(t % 2 == 0, 1.0, -1.0)

    cos_ft = (ca[:, :, None] * cb[:, None, :] - sa[:, :, None] * sb[:, None, :]).reshape(n, seq)
    sin_ft = (sa[:, :, None] * cb[:, None, :] + ca[:, :, None] * sb[:, None, :]).reshape(n, seq)
    jj = j[:, None]
    tab = jnp.where(jj < seq, cos_ft, jnp.where(jj == seq, nyq[None, :], sin_ft)).astype(BF16)
    tab = lax.optimization_barrier(tab)
    return tab, tab.T


def _filter_embedding(seq):
    pos = jnp.arange(seq, dtype=F32)[:, None]
    t = pos / max(seq - 1, 1)
    bands = jnp.linspace(1e-4, N_BANDS - 1, N_BANDS, dtype=F32)[None, :]
    ang = bands * (2.0 * math.pi) * pos / seq
    return jnp.concatenate([t, jnp.cos(ang), -jnp.sin(ang)], axis=-1)


def _filter_deltas(width):
    return jnp.abs(jnp.linspace(math.log(FILTER_TARGET) / SLOW_DECAY_PCT,
                                math.log(FILTER_TARGET) / FAST_DECAY_PCT, width, dtype=F32))


def _kspec_kernel(z_ref, w1_ref, b1_ref, fr_ref, w2_ref, b2_ref, w3f_ref, w3b_ref, dl_ref, f_ref, k_ref, h_ref):
    seq = z_ref.shape[0]
    ct = k_ref.shape[1]

    @pl.when(pl.program_id(0) == 0)
    def _():
        fr = fr_ref[...]
        h1 = jnp.sin(fr * (_dot3(z_ref[...], w1_ref[...]) + b1_ref[...]))
        h_ref[...] = jnp.sin(fr * (_dot3(h1, w2_ref[...]) + b2_ref[...]))

    h = h_ref[...]
    row = lax.broadcasted_iota(jnp.int32, (seq, ct), 0)
    t = row.astype(F32) / max(seq - 1, 1)
    window = jnp.exp(-t * dl_ref[...]) + FILTER_SHIFT
    hf = _dot3(h, w3f_ref[...]) * window
    hb = jnp.where(row == 0, 0.0, _dot3(h, w3b_ref[...]) * window)
    r = lax.rsqrt(jnp.sum(hf * hf, axis=0, keepdims=True) + jnp.sum(hb * hb, axis=0, keepdims=True))
    plus = ((hf + hb) * r).astype(BF16)
    minus = ((hf - hb) * r).astype(BF16)
    k_ref[0:seq, :] = _dot(f_ref[0:seq, :], plus)
    k_ref[seq:2 * seq, :] = _dot(f_ref[seq:2 * seq, :], minus)
    nyq = _dot(f_ref[seq:seq + SUBLANES, :], plus)
    first = lax.broadcasted_iota(jnp.int32, (SUBLANES, ct), 0) == 0
    k_ref[seq:seq + SUBLANES, :] = jnp.where(first, nyq, k_ref[seq:seq + SUBLANES, :])


def _kspec(z, w1, b1, freq, w2, b2, w3, deltas, fmat, ct):
    seq = z.shape[0]
    ch = deltas.shape[1]
    hid = w2.shape[0]
    nb = ch // ct
    full = lambda a: pl.BlockSpec(a.shape, lambda c: (0,) * a.ndim)
    return pl.pallas_call(
        _kspec_kernel,
        grid=(nb,),
        in_specs=[full(z), full(w1), full(b1), full(freq), full(w2), full(b2),
                  pl.BlockSpec((hid, ct), lambda c: (0, c)),
                  pl.BlockSpec((hid, ct), lambda c: (0, nb + c)),
                  pl.BlockSpec((1, ct), lambda c: (0, c)),
                  pl.BlockSpec(fmat.shape, lambda c: (0, 0), pipeline_mode=pl.Buffered(1))],
        out_specs=pl.BlockSpec((2 * seq, ct), lambda c: (0, c)),
        out_shape=jax.ShapeDtypeStruct((2 * seq, ch), F32),
        scratch_shapes=[pltpu.VMEM((seq, hid), F32)],
        compiler_params=_cparams(("arbitrary",)),
        name="hyena_filter_spectrum",
    )(z, w1, b1, freq, w2, b2, w3, w3, deltas, fmat)


def _short_conv(t, w, b):
    seq = t.shape[0]
    row = lax.broadcasted_iota(jnp.int32, t.shape, 0)
    prev = jnp.where(row == 0, 0.0, pltpu.roll(t, 1, 0))
    nxt = jnp.where(row == seq - 1, 0.0, pltpu.roll(t, seq - 1, 0))
    return prev * w[0:1] + t * w[1:2] + nxt * w[2:3] + b


def _hy_fwd_kernel(v_ref, x1_ref, wv_ref, wx1_ref, bv_ref, bx1_ref, f_ref, k_ref, z_ref, u_ref):
    seq = v_ref.shape[0]
    n = 2 * seq
    vc = _short_conv(v_ref[...], wv_ref[...], bv_ref[...])
    x1c = _short_conv(x1_ref[...], wx1_ref[...], bx1_ref[...])
    ub = (x1c * vc).astype(BF16)
    u_ref[...] = ub
    fs = seq // HY_SPLIT
    for r in range(HY_SPLIT):
        xa = _dot(f_ref[r * fs:(r + 1) * fs, :], ub)
        xb = _dot(f_ref[seq + r * fs:seq + (r + 1) * fs, :], ub)
        ka = k_ref[r * fs:(r + 1) * fs, :]
        kb = k_ref[seq + r * fs:seq + (r + 1) * fs, :]
        bb = xb * kb
        if r == 0:
            first = lax.broadcasted_iota(jnp.int32, xa.shape, 0) == 0
            w = jnp.where(first, 1.0 / n, 2.0 / n)
            za = (xa * ka - jnp.where(first, 0.0, bb)) * w
            zb = jnp.where(first, bb, xa * kb + xb * ka) * w
        else:
            za = (xa * ka - bb) * (2.0 / n)
            zb = (xa * kb + xb * ka) * (2.0 / n)
        z_ref[r * fs:(r + 1) * fs, :] = za.astype(z_ref.dtype)
        z_ref[seq + r * fs:seq + (r + 1) * fs, :] = zb.astype(z_ref.dtype)


def _hy_inv_kernel(z_ref, g_ref, u_ref, x0_ref, wx0_ref, bx0_ref, bias_ref, o_ref):
    seq = x0_ref.shape[0]
    x0c = _short_conv(x0_ref[...], wx0_ref[...], bx0_ref[...])
    ts = seq // HY_SPLIT
    for r in range(HY_SPLIT):
        rs = slice(r * ts, (r + 1) * ts)
        y = _dot(g_ref[rs, :], z_ref[...])
        o_ref[rs, :] = (x0c[rs] * (y + u_ref[rs, :].astype(F32) * bias_ref[...])).astype(o_ref.dtype)


def _hyena(p_hy, col0, conv_w, conv_b, kspec, bias, fmat, gmat, ct):
    b, seq, _ = p_hy.shape
    ch = bias.shape[1]
    nb = ch // ct
    taps = conv_w.shape[0]
    assert col0 % ct == 0
    act = lambda part: pl.BlockSpec((None, seq, ct), lambda c, bi: (bi, 0, col0 // ct + part * nb + c))
    cw = lambda part: pl.BlockSpec((taps, ct), lambda c, bi: (0, part * nb + c))
    cb = lambda part: pl.BlockSpec((1, ct), lambda c, bi: (0, part * nb + c))
    const = lambda a: pl.BlockSpec(a.shape, lambda c, bi: (0, 0), pipeline_mode=pl.Buffered(1))
    zspec = pl.BlockSpec((None, 2 * seq, ct), lambda c, bi: (bi, 0, c))
    uspec = pl.BlockSpec((None, seq, ct), lambda c, bi: (bi, 0, c))
    z, u = pl.pallas_call(
        _hy_fwd_kernel,
        grid=(nb, b),
        in_specs=[act(0), act(1), cw(0), cw(1), cb(0), cb(1), const(fmat),
                  pl.BlockSpec((2 * seq, ct), lambda c, bi: (0, c))],
        out_specs=[zspec, uspec],
        out_shape=[jax.ShapeDtypeStruct((b, 2 * seq, ch), BF16), jax.ShapeDtypeStruct((b, seq, ch), BF16)],
        compiler_params=_cparams(("parallel", "parallel")),
        name="hyena_forward_dft",
    )(p_hy, p_hy, conv_w, conv_w, conv_b, conv_b, fmat, kspec)
    return pl.pallas_call(
        _hy_inv_kernel,
        grid=(nb, b),
        in_specs=[zspec, const(gmat), uspec, act(2), cw(2), cb(2), pl.BlockSpec((1, ct), lambda c, bi: (0, c))],
        out_specs=pl.BlockSpec((None, seq, ct), lambda c, bi: (bi, 0, c)),
        out_shape=jax.ShapeDtypeStruct((b, seq, ch), BF16),
        compiler_params=_cparams(("parallel", "parallel")),
        name="hyena_inverse_dft",
    )(z, gmat, u, p_hy, conv_w, conv_b, bias)


def _wout_kernel(ar_ref, ac_ref, y_ref, x_ref, gate_ref, w_ref, o_ref):
    mix = jnp.concatenate([ar_ref[...], ac_ref[...], y_ref[...]], axis=1)
    o_ref[...] = x_ref[...] + gate_ref[...] * _dot(mix, w_ref[...])


def _wout(a_row, a_col, y, x2, gate, w, rows_per_mod, tm):
    t, d = x2.shape
    m = gate.shape[0]
    per = rows_per_mod // tm
    wr, wc, wy = a_row.shape[1], a_col.shape[1], y.shape[1]
    return pl.pallas_call(
        _wout_kernel,
        grid=(t // tm,),
        in_specs=[pl.BlockSpec((tm, wr), lambda i: (i, 0)),
                  pl.BlockSpec((tm, wc), lambda i: (i, 0)),
                  pl.BlockSpec((tm, wy), lambda i: (i, 0)),
                  pl.BlockSpec((tm, d), lambda i: (i, 0)),
                  pl.BlockSpec((None, 1, d), lambda i: (i // per, 0, 0)),
                  pl.BlockSpec(w.shape, lambda i: (0, 0))],
        out_specs=pl.BlockSpec((tm, d), lambda i: (i, 0)),
        out_shape=jax.ShapeDtypeStruct((t, d), F32),
        compiler_params=_cparams(("parallel",)),
        name="out_projection_residual",
    )(a_row, a_col, y, x2, gate.reshape(m, 1, d), w)


def _mlp_kernel(x_ref, g_ref, sh_ref, sc_ref, gate_ref, w1_ref, w2_ref, fg_ref, o_ref, u_ref, acc_ref):
    j = pl.program_id(1)
    last = pl.num_programs(1) - 1
    sub = u_ref.shape[0] // MLP_ROW_SPLIT

    def step(first, final):
        for r in range(MLP_ROW_SPLIT):
            rs = slice(r * sub, (r + 1) * sub)
            if first:
                y = _rms(x_ref[rs, :], g_ref[...])
                u_ref[rs, :] = (y * (1.0 + sc_ref[...]) + sh_ref[...]).astype(BF16)
            h = jnp.maximum(_dot(u_ref[rs, :], w1_ref[...]), 0.0)
            acc = _dot((h * h).astype(BF16), w2_ref[...])
            if not first:
                acc = acc_ref[rs, :] + acc
            if final:
                o_ref[rs, :] = _rms(x_ref[rs, :] + gate_ref[...] * acc, fg_ref[...])
            else:
                acc_ref[rs, :] = acc

    pl.when(j == 0)(functools.partial(step, True, False))
    pl.when((j > 0) & (j < last))(functools.partial(step, False, False))
    pl.when(j == last)(functools.partial(step, False, True))


def _mlp(x2, g, shift, scale, gate, w1, w2, fg, rows_per_mod, tm, tf):
    t, d = x2.shape
    dff = w1.shape[1]
    m = shift.shape[0]
    per = rows_per_mod // tm
    mod = pl.BlockSpec((None, 1, d), lambda i, j: (i // per, 0, 0))
    vec = pl.BlockSpec((1, d), lambda i, j: (0, 0))
    r3 = lambda a: a.reshape(m, 1, d)
    return pl.pallas_call(
        _mlp_kernel,
        grid=(t // tm, dff // tf),
        in_specs=[pl.BlockSpec((tm, d), lambda i, j: (i, 0)), vec, mod, mod, mod,
                  pl.BlockSpec((d, tf), lambda i, j: (0, j)),
                  pl.BlockSpec((tf, d), lambda i, j: (j, 0)),
                  vec],
        out_specs=pl.BlockSpec((tm, d), lambda i, j: (i, 0)),
        out_shape=jax.ShapeDtypeStruct((t, d), F32),
        scratch_shapes=[pltpu.VMEM((tm, d), BF16), pltpu.VMEM((tm, d), F32)],
        compiler_params=_cparams(("parallel", "arbitrary")),
        name="mlp_residual_final_norm",
    )(x2, g.reshape(1, d), r3(shift), r3(scale), r3(gate), w1, w2, fg.reshape(1, d))


def kernel(x, c, ctx, c_ctx, w_ada, b_ada, norm1_g, w_in, hgrn_lb_logits, hgrn_norm_g, hy_conv_w, hy_conv_b,
           flt_w1, flt_b1, flt_freq, flt_w2, flt_b2, flt_w3, hy_bias, w_out, norm2_g, w_mlp1, w_mlp2,
           final_norm_g):
    b, seq, d = x.shape
    ctx_len = ctx.shape[1]
    depth = w_ada.shape[0]
    assert depth == 1, "single-layer block"
    rows = seq // GRID_W
    hg_cols = 5 * HG_WIDTH
    hy_width = d - HG_WIDTH
    layer = 0

    pad = (-(b + 1)) % SUBLANES
    stacked = jnp.concatenate([c, c_ctx[None, :], jnp.zeros((pad, d), F32)], axis=0)
    mod = _ada(stacked, w_ada[layer], b_ada[layer])
    sh1, sc1, g1, sh2, sc2, g2 = [mod[:b, i * d:(i + 1) * d] for i in range(N_MOD)]
    csh1, csc1 = mod[b:b + 1, 0:d], mod[b:b + 1, d:2 * d]

    w_in_b = w_in[layer].astype(BF16)
    x2 = x.reshape(b * seq, d)
    ctx2 = ctx.reshape(b * ctx_len, d)
    p_lat, (w_out_b, w_mlp1_b, w_mlp2_b) = _norm_mod_matmul_pipelined(
        x2, norm1_g[layer], sh1, sc1, w_in_b, seq, 1024, 2048, "in_projection",
        side_casts=(w_out[layer], w_mlp1[layer], w_mlp2[layer]))
    p_lat = p_lat.reshape(b, seq, -1)
    p_ctx = _norm_mod_matmul(ctx2, norm1_g[layer], csh1, csc1, w_in_b[:, HG_WIDTH:4 * HG_WIDTH], b * ctx_len,
                             F32, min(1024, b * ctx_len), 1024, "in_projection_context")
    p_ctx = p_ctx.reshape(b, ctx_len, 3 * HG_WIDTH)

    lbl = hgrn_lb_logits.astype(F32).reshape(2 * (depth + 1), HG_WIDTH)
    gain = hgrn_norm_g[layer].reshape(1, HG_WIDTH)
    a_row = _hgrn(p_lat, p_ctx, lbl, gain, 0, HG_ROW_HEADS, False, rows, "hgrn_row_heads")
    a_col = _hgrn(p_lat, p_ctx, lbl, gain, HG_ROW_HEADS, HG_HEADS - HG_ROW_HEADS, True, rows, "hgrn_col_heads")

    fmat, gmat = _dft_tables(seq)
    emb = _filter_embedding(seq)
    kpad = (-emb.shape[1]) % LANES
    emb = jnp.pad(emb, ((0, 0), (0, kpad)))
    fw1 = jnp.pad(flt_w1[layer], ((0, kpad), (0, 0)))
    row = lambda a: a[layer].reshape(1, -1)
    kspec = _kspec(emb, fw1, row(flt_b1), row(flt_freq), flt_w2[layer], row(flt_b2), flt_w3[layer],
                   _filter_deltas(hy_width)[None, :], fmat, 256)
    y_lat = _hyena(p_lat, hg_cols, hy_conv_w[layer], hy_conv_b[layer].reshape(1, -1), kspec, row(hy_bias),
                   fmat, gmat, 256)

    x_mid = _wout(a_row.reshape(b * seq, -1), a_col.reshape(b * seq, -1), y_lat.reshape(b * seq, hy_width),
                  x2, g1, w_out_b, seq, 512)
    out = _mlp(x_mid, norm2_g[layer], sh2, sc2, g2, w_mlp1_b, w_mlp2_b, final_norm_g, seq, 512, 1024)
    return out.reshape(b, seq, d)
```

```python
import functools
import math

import jax
import jax.numpy as jnp
from jax import lax
from jax.experimental import pallas as pl
from jax.experimental.pallas import tpu as pltpu

GRID_W = 64
HG_WIDTH = 1024
HG_HEAD_DIM = 128
HG_HEADS = HG_WIDTH // HG_HEAD_DIM
HG_ROW_HEADS = HG_HEADS // 2
CHUNK = 64
N_BANDS = 16
FILTER_TARGET = 1e-2
FAST_DECAY_PCT = 0.3
SLOW_DECAY_PCT = 1.5
FILTER_SHIFT = 0.05
N_MOD = 6
EPS = 1e-6

F32 = jnp.float32
BF16 = jnp.bfloat16

V7X_VMEM_BYTES = 64 * 1024 * 1024
VMEM_LIMIT = 56 * 1024 * 1024
LANES = 128
SUBLANES = 8

SUPER = 256
N_SUB = SUPER // CHUNK

MLP_ROW_SPLIT = 2
HY_SPLIT = 2


def _cparams(sem, flags=None):
    return pltpu.CompilerParams(dimension_semantics=sem, vmem_limit_bytes=VMEM_LIMIT, flags=flags)


def _split_bf16(a):
    hi = a.astype(BF16)
    lo = (a - hi.astype(F32)).astype(BF16)
    return hi, lo


def _dot(a, b):
    return jnp.dot(a, b, preferred_element_type=F32)


def _dot_nt(a, b):
    return lax.dot_general(a, b, (((1,), (1,)), ((), ())), preferred_element_type=F32)


def _dot_tn(a, b):
    return lax.dot_general(a, b, (((0,), (0,)), ((), ())), preferred_element_type=F32)


def _dot3(a, b):
    ah, al = _split_bf16(a)
    bh, bl = _split_bf16(b)
    return _dot(ah, bh) + _dot(ah, bl) + _dot(al, bh)


def _rms(x, g):
    return x * lax.rsqrt(jnp.mean(x * x, axis=-1, keepdims=True) + EPS) * g


def _ilog2(n):
    assert n > 0 and n & (n - 1) == 0, n
    return n.bit_length() - 1


def _ada_kernel(s_ref, w_ref, b_ref, o_ref):
    s = s_ref[...]
    s = s * jax.nn.sigmoid(s)
    o_ref[...] = _dot3(s, w_ref[...]) + b_ref[...]


def _ada(stacked, w, b):
    r, d = stacked.shape
    n = w.shape[1]
    tn = 512
    return pl.pallas_call(
        _ada_kernel,
        grid=(n // tn,),
        in_specs=[pl.BlockSpec((r, d), lambda j: (0, 0)),
                  pl.BlockSpec((d, tn), lambda j: (0, j)),
                  pl.BlockSpec((1, tn), lambda j: (0, j))],
        out_specs=pl.BlockSpec((r, tn), lambda j: (0, j)),
        out_shape=jax.ShapeDtypeStruct((r, n), F32),
        compiler_params=_cparams(("parallel",)),
        name="ada_modulation",
    )(stacked, w, b.reshape(1, n))


def _nmm_kernel(x_ref, g_ref, sh_ref, sc_ref, w_ref, o_ref, u_ref):
    @pl.when(pl.program_id(1) == 0)
    def _():
        y = _rms(x_ref[...], g_ref[...])
        u_ref[...] = (y * (1.0 + sc_ref[...]) + sh_ref[...]).astype(BF16)

    o_ref[...] = _dot(u_ref[...], w_ref[...]).astype(o_ref.dtype)


def _norm_mod_matmul(x2, g, shift, scale, w, rows_per_mod, out_dtype, tm, tn, name):
    t, d = x2.shape
    n = w.shape[1]
    m = shift.shape[0]
    per = rows_per_mod // tm
    mod_map = lambda i, j: (i // per, 0, 0)
    return pl.pallas_call(
        _nmm_kernel,
        grid=(t // tm, n // tn),
        in_specs=[pl.BlockSpec((tm, d), lambda i, j: (i, 0)),
                  pl.BlockSpec((1, d), lambda i, j: (0, 0)),
                  pl.BlockSpec((None, 1, d), mod_map),
                  pl.BlockSpec((None, 1, d), mod_map),
                  pl.BlockSpec((d, tn), lambda i, j: (0, j))],
        out_specs=pl.BlockSpec((tm, tn), lambda i, j: (i, j)),
        out_shape=jax.ShapeDtypeStruct((t, n), out_dtype),
        scratch_shapes=[pltpu.VMEM((tm, d), BF16)],
        compiler_params=_cparams(("parallel", "arbitrary")),
        name=name,
    )(x2, g.reshape(1, d), shift.reshape(m, 1, d), scale.reshape(m, 1, d), w)


def _nmm_pipelined_kernel(*refs, n_side):
    xs_ref, g_ref, sh_ref, sc_ref, w_ref = refs[:5]
    side_in = refs[5:5 + n_side]
    o_ref = refs[5 + n_side]
    side_out = refs[6 + n_side:6 + 2 * n_side]
    ua_ref, ub_ref = refs[6 + 2 * n_side:]
    i = pl.program_id(0)
    j = pl.program_id(1)
    ts = xs_ref.shape[0]

    def body(fill_ref, use_ref):
        y = _rms(xs_ref[...], g_ref[...])
        fill_ref[pl.ds(pl.multiple_of(j * ts, ts), ts), :] = (y * (1.0 + sc_ref[...]) + sh_ref[...]).astype(BF16)
        for src, dst in zip(side_in, side_out):
            dst[...] = src[...].astype(BF16)
        if use_ref is not None:
            o_ref[...] = _dot(use_ref[...], w_ref[...])
        else:
            o_ref[...] = jnp.zeros_like(o_ref)

    pl.when(i == 0)(functools.partial(body, ua_ref, None))
    pl.when((i > 0) & (i % 2 == 0))(functools.partial(body, ua_ref, ub_ref))
    pl.when(i % 2 == 1)(functools.partial(body, ub_ref, ua_ref))


def _norm_mod_matmul_pipelined(x2, g, shift, scale, w, rows_per_mod, tm, tn, name, side_casts=()):
    t, d = x2.shape
    n = w.shape[1]
    m = shift.shape[0]
    per = rows_per_mod // tm
    nt, nc = t // tm, n // tn
    ts = tm // nc
    assert ts * nc == tm and ts % SUBLANES == 0
    steps = (nt + 1) * nc
    cur = lambda i: jnp.minimum(i, nt - 1)
    mod_map = lambda i, j: (cur(i) // per, 0, 0)

    side_specs = []
    for a in side_casts:
        rows = a.shape[0]
        br = next(r for r in range(2 * SUBLANES, rows + 1, 2 * SUBLANES) if rows % r == 0 and rows // r <= steps)
        nblk = rows // br
        side_specs.append(pl.BlockSpec((br, a.shape[1]),
                                       lambda i, j, nblk=nblk: (jnp.minimum(i * nc + j, nblk - 1), 0)))

    outs = pl.pallas_call(
        functools.partial(_nmm_pipelined_kernel, n_side=len(side_casts)),
        grid=(nt + 1, nc),
        in_specs=[pl.BlockSpec((ts, d), lambda i, j: (cur(i) * nc + j, 0)),
                  pl.BlockSpec((1, d), lambda i, j: (0, 0)),
                  pl.BlockSpec((None, 1, d), mod_map),
                  pl.BlockSpec((None, 1, d), mod_map),
                  pl.BlockSpec((d, tn), lambda i, j: (0, j))] + side_specs,
        out_specs=[pl.BlockSpec((tm, tn), lambda i, j: (jnp.maximum(i - 1, 0), jnp.where(i == 0, 0, j)))]
        + side_specs,
        out_shape=[jax.ShapeDtypeStruct((t, n), F32)] + [jax.ShapeDtypeStruct(a.shape, BF16) for a in side_casts],
        scratch_shapes=[pltpu.VMEM((tm, d), BF16), pltpu.VMEM((tm, d), BF16)],
        compiler_params=_cparams(("arbitrary", "arbitrary")),
        name=name,
    )(x2, g.reshape(1, d), shift.reshape(m, 1, d), scale.reshape(m, 1, d), w, *side_casts)
    return outs[0], tuple(outs[1:])


def _scan_pos(t, col_order, rows):
    if not col_order:
        return t
    cols = SUPER // rows
    return jnp.bitwise_and(t, cols - 1) * rows + jnp.right_shift(t, _ilog2(cols))


def _scan_constants(orders, rows):
    lc = _ilog2(CHUNK)
    half = CHUNK // 2
    dms, masks, rsels, cms = [], [], [], []
    for col_order in orders:
        pi = _scan_pos(lax.broadcasted_iota(jnp.int32, (SUPER, SUPER), 0), col_order, rows)
        pj = _scan_pos(lax.broadcasted_iota(jnp.int32, (SUPER, SUPER), 1), col_order, rows)
        same = jnp.right_shift(pi, lc) == jnp.right_shift(pj, lc)
        wj = jnp.bitwise_and(pj, CHUNK - 1)
        c8 = lax.broadcasted_iota(jnp.int32, (2 * N_SUB, SUPER), 0)
        p8 = _scan_pos(lax.broadcasted_iota(jnp.int32, (2 * N_SUB, SUPER), 1), col_order, rows)
        in_chunk = jnp.right_shift(p8, lc) == jnp.bitwise_and(c8, N_SUB - 1)
        w8 = jnp.bitwise_and(p8, CHUNK - 1)
        for fwd in (True, False):
            incl = same & ((pj <= pi) if fwd else (pj >= pi))
            upto_mid = same & ((wj < half) if fwd else (wj >= half))
            dms.append((incl.astype(F32) - upto_mid.astype(F32)).astype(BF16))
            masks.append(incl.astype(F32))
            sel = in_chunk & ((c8 >= N_SUB) | ((w8 < half) if fwd else (w8 >= half)))
            rsels.append(sel.astype(BF16))
        pr = jnp.right_shift(
            _scan_pos(lax.broadcasted_iota(jnp.int32, (SUPER, HG_HEAD_DIM), 0), col_order, rows), lc)
        cms.append(jnp.stack([(pr == c).astype(BF16) for c in range(N_SUB)]))
    return jnp.stack(dms), jnp.stack(masks), jnp.stack(rsels), jnp.stack(cms)


def _hgrn_kernel(q_ref, zf_ref, zb_ref, v_ref, g_ref, czf_ref, czb_ref, cv_ref, lbl_ref, gain_ref,
                 dm_ref, mask_ref, rsel_ref, cm_ref, o_ref, acc_ref, st_ref, *stage, col_order, rows):
    seq, width = q_ref.shape
    ctx_len = cv_ref.shape[0]
    heads = width // HG_HEAD_DIM
    n_lat = seq // SUPER
    n_ctx = ctx_len // SUPER
    n_slots = lbl_ref.shape[0] // 2
    cols = SUPER // rows
    d = HG_HEAD_DIM
    lat_slot = cm_ref.shape[0] - 1

    def lower_bound(direction):
        lg = lbl_ref[direction * n_slots:(direction + 1) * n_slots, :]
        e = jnp.exp(lg - jnp.max(lg, axis=0, keepdims=True))
        return e[0:1] / jnp.sum(e, axis=0, keepdims=True)

    lbs = (lower_bound(0), lower_bound(1))
    gain = gain_ref[...]

    def row_starts(sc, order):
        if not order:
            return [(sc * SUPER, SUPER)]
        return [(r * GRID_W + sc * cols, cols) for r in range(rows)]

    def load_rows(ref, sc, order):
        parts = [ref[pl.ds(s, n), :] for s, n in row_starts(sc, order)]
        return parts[0] if len(parts) == 1 else jnp.concatenate(parts, axis=0)

    def store_rows(ref, sc, order, val):
        off = 0
        for s, n in row_starts(sc, order):
            ref[pl.ds(s, n), :] = val[off:off + n]
            off += n

    def hsl(h):
        return slice(h * d, (h + 1) * d)

    def phase_a(sc, di, slot, order, q_r, z_r, v_r, bufs):
        qsub_ref, upd_ref, oin_ref, sums_ref = bufs
        k_idx = 2 * slot + di
        z = load_rows(z_r, sc, order)
        lb = lbs[di]
        f = lb + (1.0 - lb) * jax.nn.sigmoid(z)
        k = 1.0 - f
        lf_hi, lf_lo = _split_bf16(jnp.log(f))
        dm = dm_ref[k_idx]
        x1 = _dot(dm, lf_hi) + _dot(dm, lf_lo)
        rsel = rsel_ref[k_idx]
        sums_ref[...] = _dot(rsel, lf_hi) + _dot(rsel, lf_lo)
        ks = (k * jnp.exp(-x1)).astype(BF16)
        vb = load_rows(v_r, sc, order).astype(BF16)
        if q_r is not None:
            qs = (load_rows(q_r, sc, order) * jnp.exp(x1)).astype(BF16)
            allowed = mask_ref[k_idx] != 0.0
        for h in range(heads):
            sl = hsl(h)
            ks_h = ks[:, sl]
            vb_h = vb[:, sl]
            ks_sub = jnp.concatenate([ks_h * cm_ref[slot, c] for c in range(N_SUB)], axis=1)
            upd_ref[h] = _dot_tn(vb_h, ks_sub)
            if q_r is not None:
                qs_h = qs[:, sl]
                s = jnp.where(allowed, _dot_nt(qs_h, ks_h), 0.0).astype(BF16)
                oin_ref[:, sl] = _dot(s, vb_h)
                qsub_ref[:, h * N_SUB * d:(h + 1) * N_SUB * d] = jnp.concatenate(
                    [qs_h * cm_ref[slot, c] for c in range(N_SUB)], axis=1)

    def phase_b(sc, di, order, bufs, with_out):
        qsub_ref, upd_ref, oin_ref, sums_ref = bufs
        fwd = di == 0
        sums = sums_ref[...]
        a_mid = sums[0:N_SUB]
        a_end = sums[N_SUB:2 * N_SUB]
        upd_scale = jnp.exp(a_end - a_mid)
        decay = jnp.exp(a_end)
        mid_scale = jnp.exp(a_mid)
        outs = []
        for h in range(heads):
            sl = hsl(h)
            upd = upd_ref[h]
            st = st_ref[di * heads + h]
            entering = [None] * N_SUB
            for c in (range(N_SUB) if fwd else reversed(range(N_SUB))):
                entering[c] = st * mid_scale[c:c + 1, sl]
                st = st * decay[c:c + 1, sl] + upd[:, c * d:(c + 1) * d] * upd_scale[c:c + 1, sl]
            st_ref[di * heads + h] = st
            if with_out:
                st_cat = jnp.concatenate(entering, axis=1).astype(BF16)
                outs.append(oin_ref[:, sl] + _dot_nt(qsub_ref[:, h * N_SUB * d:(h + 1) * N_SUB * d], st_cat))
        if with_out:
            store_rows(acc_ref.at[di], sc, order, jnp.concatenate(outs, axis=1))

    def bufs(di, par):
        base = (2 * di + par) * 4
        return stage[base:base + 4]

    def readout(sc):
        o = load_rows(acc_ref.at[0], sc, col_order) + load_rows(acc_ref.at[1], sc, col_order)
        g = load_rows(g_ref, sc, col_order)
        res = []
        for h in range(heads):
            oh = o[:, hsl(h)]
            res.append(oh * lax.rsqrt(jnp.mean(oh * oh, axis=-1, keepdims=True) + EPS))
        store_rows(acc_ref.at[0], sc, col_order, jnp.concatenate(res, axis=1) * gain * (g * jax.nn.sigmoid(g)))

    st_ref[...] = jnp.zeros_like(st_ref)
    for i in range(n_ctx):
        for di, cz_r in ((0, czf_ref), (1, czb_ref)):
            sc = i if di == 0 else n_ctx - 1 - i
            phase_a(sc, di, 0, False, None, cz_r, cv_ref, bufs(di, 0))
            phase_b(sc, di, False, bufs(di, 0), False)

    z_refs = (zf_ref, zb_ref)
    for k in range(n_lat + 1):
        if k < n_lat:
            for di in range(2):
                phase_a(k if di == 0 else n_lat - 1 - k, di, lat_slot, col_order, q_ref, z_refs[di], v_ref,
                        bufs(di, k % 2))
        if k >= 1:
            for di in range(2):
                phase_b(k - 1 if di == 0 else n_lat - k, di, col_order, bufs(di, (k - 1) % 2), True)
            if 2 * (k - 1) >= n_lat - 1:
                for sc in sorted({k - 1, n_lat - k}):
                    readout(sc)
    o_ref[...] = acc_ref[0].astype(o_ref.dtype)


def _hgrn(p_hg, p_ctx, lb_logits, gain, head0, n_heads, col_order, rows, name):
    b, seq, _ = p_hg.shape
    ctx_len = p_ctx.shape[1]
    assert seq % SUPER == 0 and ctx_len % SUPER == 0 and 2 * N_SUB == SUBLANES
    assert not col_order or (rows * SUBLANES == SUPER and GRID_W % SUBLANES == 0)
    hps = 2
    width = hps * HG_HEAD_DIM
    groups = n_heads // hps
    part_blocks = HG_WIDTH // width
    g0 = head0 // hps
    n_rows = lb_logits.shape[0]
    consts = _scan_constants((False, True) if col_order else (False,), rows)

    def slab(length, part):
        return pl.BlockSpec((None, length, width), lambda bi, gi: (bi, 0, part * part_blocks + g0 + gi))

    whole = lambda a: pl.BlockSpec(a.shape, lambda bi, gi: (0,) * a.ndim)
    stage_bufs = [pltpu.VMEM((SUPER, hps * N_SUB * HG_HEAD_DIM), BF16),
                  pltpu.VMEM((hps, HG_HEAD_DIM, N_SUB * HG_HEAD_DIM), F32),
                  pltpu.VMEM((SUPER, width), F32),
                  pltpu.VMEM((2 * N_SUB, width), F32)]
    assert (seq // SUPER) % 2 == 0
    kern = functools.partial(_hgrn_kernel, col_order=col_order, rows=rows)
    return pl.pallas_call(
        kern,
        grid=(b, groups),
        in_specs=[slab(seq, 0), slab(seq, 1), slab(seq, 2), slab(seq, 3), slab(seq, 4),
                  slab(ctx_len, 0), slab(ctx_len, 1), slab(ctx_len, 2),
                  pl.BlockSpec((n_rows, width), lambda bi, gi: (0, g0 + gi)),
                  pl.BlockSpec((1, width), lambda bi, gi: (0, g0 + gi))] + [whole(a) for a in consts],
        out_specs=pl.BlockSpec((None, seq, width), lambda bi, gi: (bi, 0, gi)),
        out_shape=jax.ShapeDtypeStruct((b, seq, n_heads * HG_HEAD_DIM), BF16),
        scratch_shapes=[pltpu.VMEM((2, seq, width), F32),
                        pltpu.VMEM((2 * hps, HG_HEAD_DIM, HG_HEAD_DIM), F32)] + stage_bufs * 4,
        compiler_params=_cparams(("parallel", "parallel")),
        name=name,
    )(p_hg, p_hg, p_hg, p_hg, p_hg, p_ctx, p_ctx, p_ctx, lb_logits, gain, *consts)


def _dft_tables(seq):
    n = 2 * seq
    lo = GRID_W
    j = jnp.arange(n, dtype=jnp.int32)
    f = jnp.where(j < seq, j, j - seq)
    ang = lambda m: (m % n).astype(F32) * (2.0 * math.pi / n)
    a = ang(f[:, None] * (jnp.arange(seq // lo, dtype=jnp.int32) * lo)[None, :])
    b = ang(f[:, None] * jnp.arange(lo, dtype=jnp.int32)[None, :])
    ca, sa, cb, sb = jnp.cos(a), jnp.sin(a), jnp.cos(b), jnp.sin(b)
    t = jnp.arange(seq, dtype=jnp.int32)
    nyq = jnp.where(t % 2 == 0, 1.0, -1.0)

    cos_ft = (ca[:, :, None] * cb[:, None, :] - sa[:, :, None] * sb[:, None, :]).reshape(n, seq)
    sin_ft = (sa[:, :, None] * cb[:, None, :] + ca[:, :, None] * sb[:, None, :]).reshape(n, seq)
    jj = j[:, None]
    tab = jnp.where(jj < seq, cos_ft, jnp.where(jj == seq, nyq[None, :], sin_ft)).astype(BF16)
    tab = lax.optimization_barrier(tab)
    return tab, tab.T


def _filter_embedding(seq):
    pos = jnp.arange(seq, dtype=F32)[:, None]
    t = pos / max(seq - 1, 1)
    bands = jnp.linspace(1e-4, N_BANDS - 1, N_BANDS, dtype=F32)[None, :]
    ang = bands * (2.0 * math.pi) * pos / seq
    return jnp.concatenate([t, jnp.cos(ang), -jnp.sin(ang)], axis=-1)


def _filter_deltas(width):
    return jnp.abs(jnp.linspace(math.log(FILTER_TARGET) / SLOW_DECAY_PCT,
                                math.log(FILTER_TARGET) / FAST_DECAY_PCT, width, dtype=F32))


def _kspec_kernel(z_ref, w1_ref, b1_ref, fr_ref, w2_ref, b2_ref, w3f_ref, w3b_ref, dl_ref, f_ref, k_ref, h_ref):
    seq = z_ref.shape[0]
    ct = k_ref.shape[1]

    @pl.when(pl.program_id(0) == 0)
    def _():
        fr = fr_ref[...]
        h1 = jnp.sin(fr * (_dot3(z_ref[...], w1_ref[...]) + b1_ref[...]))
        h_ref[...] = jnp.sin(fr * (_dot3(h1, w2_ref[...]) + b2_ref[...]))

    h = h_ref[...]
    row = lax.broadcasted_iota(jnp.int32, (seq, ct), 0)
    t = row.astype(F32) / max(seq - 1, 1)
    window = jnp.exp(-t * dl_ref[...]) + FILTER_SHIFT
    hf = _dot3(h, w3f_ref[...]) * window
    hb = jnp.where(row == 0, 0.0, _dot3(h, w3b_ref[...]) * window)
    r = lax.rsqrt(jnp.sum(hf * hf, axis=0, keepdims=True) + jnp.sum(hb * hb, axis=0, keepdims=True))
    plus = ((hf + hb) * r).astype(BF16)
    minus = ((hf - hb) * r).astype(BF16)
    k_ref[0:seq, :] = _dot(f_ref[0:seq, :], plus)
    k_ref[seq:2 * seq, :] = _dot(f_ref[seq:2 * seq, :], minus)
    nyq = _dot(f_ref[seq:seq + SUBLANES, :], plus)
    first = lax.broadcasted_iota(jnp.int32, (SUBLANES, ct), 0) == 0
    k_ref[seq:seq + SUBLANES, :] = jnp.where(first, nyq, k_ref[seq:seq + SUBLANES, :])


def _kspec(z, w1, b1, freq, w2, b2, w3, deltas, fmat, ct):
    seq = z.shape[0]
    ch = deltas.shape[1]
    hid = w2.shape[0]
    nb = ch // ct
    full = lambda a: pl.BlockSpec(a.shape, lambda c: (0,) * a.ndim)
    return pl.pallas_call(
        _kspec_kernel,
        grid=(nb,),
        in_specs=[full(z), full(w1), full(b1), full(freq), full(w2), full(b2),
                  pl.BlockSpec((hid, ct), lambda c: (0, c)),
                  pl.BlockSpec((hid, ct), lambda c: (0, nb + c)),
                  pl.BlockSpec((1, ct), lambda c: (0, c)),
                  pl.BlockSpec(fmat.shape, lambda c: (0, 0), pipeline_mode=pl.Buffered(1))],
        out_specs=pl.BlockSpec((2 * seq, ct), lambda c: (0, c)),
        out_shape=jax.ShapeDtypeStruct((2 * seq, ch), F32),
        scratch_shapes=[pltpu.VMEM((seq, hid), F32)],
        compiler_params=_cparams(("arbitrary",)),
        name="hyena_filter_spectrum",
    )(z, w1, b1, freq, w2, b2, w3, w3, deltas, fmat)


def _short_conv(t, w, b):
    seq = t.shape[0]
    row = lax.broadcasted_iota(jnp.int32, t.shape, 0)
    prev = jnp.where(row == 0, 0.0, pltpu.roll(t, 1, 0))
    nxt = jnp.where(row == seq - 1, 0.0, pltpu.roll(t, seq - 1, 0))
    return prev * w[0:1] + t * w[1:2] + nxt * w[2:3] + b


def _hy_fwd_kernel(v_ref, x1_ref, wv_ref, wx1_ref, bv_ref, bx1_ref, f_ref, k_ref, z_ref, u_ref):
    seq = v_ref.shape[0]
    n = 2 * seq
    vc = _short_conv(v_ref[...], wv_ref[...], bv_ref[...])
    x1c = _short_conv(x1_ref[...], wx1_ref[...], bx1_ref[...])
    ub = (x1c * vc).astype(BF16)
    u_ref[...] = ub
    fs = seq // HY_SPLIT
    for r in range(HY_SPLIT):
        xa = _dot(f_ref[r * fs:(r + 1) * fs, :], ub)
        xb = _dot(f_ref[seq + r * fs:seq + (r + 1) * fs, :], ub)
        ka = k_ref[r * fs:(r + 1) * fs, :]
        kb = k_ref[seq + r * fs:seq + (r + 1) * fs, :]
        bb = xb * kb
        if r == 0:
            first = lax.broadcasted_iota(jnp.int32, xa.shape, 0) == 0
            w = jnp.where(first, 1.0 / n, 2.0 / n)
            za = (xa * ka - jnp.where(first, 0.0, bb)) * w
            zb = jnp.where(first, bb, xa * kb + xb * ka) * w
        else:
            za = (xa * ka - bb) * (2.0 / n)
            zb = (xa * kb + xb * ka) * (2.0 / n)
        z_ref[r * fs:(r + 1) * fs, :] = za.astype(z_ref.dtype)
        z_ref[seq + r * fs:seq + (r + 1) * fs, :] = zb.astype(z_ref.dtype)


def _hy_inv_kernel(z_ref, g_ref, u_ref, x0_ref, wx0_ref, bx0_ref, bias_ref, o_ref):
    seq = x0_ref.shape[0]
    x0c = _short_conv(x0_ref[...], wx0_ref[...], bx0_ref[...])
    ts = seq // HY_SPLIT
    for r in range(HY_SPLIT):
        rs = slice(r * ts, (r + 1) * ts)
        y = _dot(g_ref[rs, :], z_ref[...])
        o_ref[rs, :] = (x0c[rs] * (y + u_ref[rs, :].astype(F32) * bias_ref[...])).astype(o_ref.dtype)


def _hyena(p_hy, col0, conv_w, conv_b, kspec, bias, fmat, gmat, ct):
    b, seq, _ = p_hy.shape
    ch = bias.shape[1]
    nb = ch // ct
    taps = conv_w.shape[0]
    assert col0 % ct == 0
    act = lambda part: pl.BlockSpec((None, seq, ct), lambda c, bi: (bi, 0, col0 // ct + part * nb + c))
    cw = lambda part: pl.BlockSpec((taps, ct), lambda c, bi: (0, part * nb + c))
    cb = lambda part: pl.BlockSpec((1, ct), lambda c, bi: (0, part * nb + c))
    const = lambda a: pl.BlockSpec(a.shape, lambda c, bi: (0, 0), pipeline_mode=pl.Buffered(1))
    zspec = pl.BlockSpec((None, 2 * seq, ct), lambda c, bi: (bi, 0, c))
    uspec = pl.BlockSpec((None, seq, ct), lambda c, bi: (bi, 0, c))
    z, u = pl.pallas_call(
        _hy_fwd_kernel,
        grid=(nb, b),
        in_specs=[act(0), act(1), cw(0), cw(1), cb(0), cb(1), const(fmat),
                  pl.BlockSpec((2 * seq, ct), lambda c, bi: (0, c))],
        out_specs=[zspec, uspec],
        out_shape=[jax.ShapeDtypeStruct((b, 2 * seq, ch), BF16), jax.ShapeDtypeStruct((b, seq, ch), BF16)],
        compiler_params=_cparams(("parallel", "parallel")),
        name="hyena_forward_dft",
    )(p_hy, p_hy, conv_w, conv_w, conv_b, conv_b, fmat, kspec)
    return pl.pallas_call(
        _hy_inv_kernel,
        grid=(nb, b),
        in_specs=[zspec, const(gmat), uspec, act(2), cw(2), cb(2), pl.BlockSpec((1, ct), lambda c, bi: (0, c))],
        out_specs=pl.BlockSpec((None, seq, ct), lambda c, bi: (bi, 0, c)),
        out_shape=jax.ShapeDtypeStruct((b, seq, ch), BF16),
        compiler_params=_cparams(("parallel", "parallel")),
        name="hyena_inverse_dft",
    )(z, gmat, u, p_hy, conv_w, conv_b, bias)


def _wout_kernel(ar_ref, ac_ref, y_ref, x_ref, gate_ref, w_ref, o_ref):
    mix = jnp.concatenate([ar_ref[...], ac_ref[...], y_ref[...]], axis=1)
    o_ref[...] = x_ref[...] + gate_ref[...] * _dot(mix, w_ref[...])


def _wout(a_row, a_col, y, x2, gate, w, rows_per_mod, tm):
    t, d = x2.shape
    m = gate.shape[0]
    per = rows_per_mod // tm
    wr, wc, wy = a_row.shape[1], a_col.shape[1], y.shape[1]
    return pl.pallas_call(
        _wout_kernel,
        grid=(t // tm,),
        in_specs=[pl.BlockSpec((tm, wr), lambda i: (i, 0)),
                  pl.BlockSpec((tm, wc), lambda i: (i, 0)),
                  pl.BlockSpec((tm, wy), lambda i: (i, 0)),
                  pl.BlockSpec((tm, d), lambda i: (i, 0)),
                  pl.BlockSpec((None, 1, d), lambda i: (i // per, 0, 0)),
                  pl.BlockSpec(w.shape, lambda i: (0, 0))],
        out_specs=pl.BlockSpec((tm, d), lambda i: (i, 0)),
        out_shape=jax.ShapeDtypeStruct((t, d), F32),
        compiler_params=_cparams(("parallel",)),
        name="out_projection_residual",
    )(a_row, a_col, y, x2, gate.reshape(m, 1, d), w)


def _mlp_kernel(x_ref, g_ref, sh_ref, sc_ref, gate_ref, w1_ref, w2_ref, fg_ref, o_ref, u_ref, acc_ref):
    j = pl.program_id(1)
    last = pl.num_programs(1) - 1
    sub = u_ref.shape[0] // MLP_ROW_SPLIT

    def step(first, final):
        for r in range(MLP_ROW_SPLIT):
            rs = slice(r * sub, (r + 1) * sub)
            if first:
                y = _rms(x_ref[rs, :], g_ref[...])
                u_ref[rs, :] = (y * (1.0 + sc_ref[...]) + sh_ref[...]).astype(BF16)
            h = jnp.maximum(_dot(u_ref[rs, :], w1_ref[...]), 0.0)
            acc = _dot((h * h).astype(BF16), w2_ref[...])
            if not first:
                acc = acc_ref[rs, :] + acc
            if final:
                o_ref[rs, :] = _rms(x_ref[rs, :] + gate_ref[...] * acc, fg_ref[...])
            else:
                acc_ref[rs, :] = acc

    pl.when(j == 0)(functools.partial(step, True, False))
    pl.when((j > 0) & (j < last))(functools.partial(step, False, False))
    pl.when(j == last)(functools.partial(step, False, True))


def _mlp(x2, g, shift, scale, gate, w1, w2, fg, rows_per_mod, tm, tf):
    t, d = x2.shape
    dff = w1.shape[1]
    m = shift.shape[0]
    per = rows_per_mod // tm
    mod = pl.BlockSpec((None, 1, d), lambda i, j: (i // per, 0, 0))
    vec = pl.BlockSpec((1, d), lambda i, j: (0, 0))
    r3 = lambda a: a.reshape(m, 1, d)
    return pl.pallas_call(
        _mlp_kernel,
        grid=(t // tm, dff // tf),
        in_specs=[pl.BlockSpec((tm, d), lambda i, j: (i, 0)), vec, mod, mod, mod,
                  pl.BlockSpec((d, tf), lambda i, j: (0, j)),
                  pl.BlockSpec((tf, d), lambda i, j: (j, 0)),
                  vec],
        out_specs=pl.BlockSpec((tm, d), lambda i, j: (i, 0)),
        out_shape=jax.ShapeDtypeStruct((t, d), F32),
        scratch_shapes=[pltpu.VMEM((tm, d), BF16), pltpu.VMEM((tm, d), F32)],
        compiler_params=_cparams(("parallel", "arbitrary")),
        name="mlp_residual_final_norm",
    )(x2, g.reshape(1, d), r3(shift), r3(scale), r3(gate), w1, w2, fg.reshape(1, d))


def kernel(x, c, ctx, c_ctx, w_ada, b_ada, norm1_g, w_in, hgrn_lb_logits, hgrn_norm_g, hy_conv_w, hy_conv_b,
           flt_w1, flt_b1, flt_freq, flt_w2, flt_b2, flt_w3, hy_bias, w_out, norm2_g, w_mlp1, w_mlp2,
           final_norm_g):
    b, seq, d = x.shape
    ctx_len = ctx.shape[1]
    depth = w_ada.shape[0]
    assert depth == 1, "single-layer block"
    rows = seq // GRID_W
    hg_cols = 5 * HG_WIDTH
    hy_width = d - HG_WIDTH
    layer = 0

    pad = (-(b + 1)) % SUBLANES
    stacked = jnp.concatenate([c, c_ctx[None, :], jnp.zeros((pad, d), F32)], axis=0)
    mod = _ada(stacked, w_ada[layer], b_ada[layer])
    sh1, sc1, g1, sh2, sc2, g2 = [mod[:b, i * d:(i + 1) * d] for i in range(N_MOD)]
    csh1, csc1 = mod[b:b + 1, 0:d], mod[b:b + 1, d:2 * d]

    w_in_b = w_in[layer].astype(BF16)
    x2 = x.reshape(b * seq, d)
    ctx2 = ctx.reshape(b * ctx_len, d)
    p_lat, (w_out_b, w_mlp1_b, w_mlp2_b) = _norm_mod_matmul_pipelined(
        x2, norm1_g[layer], sh1, sc1, w_in_b, seq, 1024, 2048, "in_projection",
        side_casts=(w_out[layer], w_mlp1[layer], w_mlp2[layer]))
    p_lat = p_lat.reshape(b, seq, -1)
    p_ctx = _norm_mod_matmul(ctx2, norm1_g[layer], csh1, csc1, w_in_b[:, HG_WIDTH:4 * HG_WIDTH], b * ctx_len,
                             F32, min(1024, b * ctx_len), 1024, "in_projection_context")
    p_ctx = p_ctx.reshape(b, ctx_len, 3 * HG_WIDTH)

    lbl = hgrn_lb_logits.astype(F32).reshape(2 * (depth + 1), HG_WIDTH)
    gain = hgrn_norm_g[layer].reshape(1, HG_WIDTH)
    a_row = _hgrn(p_lat, p_ctx, lbl, gain, 0, HG_ROW_HEADS, False, rows, "hgrn_row_heads")
    a_col = _hgrn(p_lat, p_ctx, lbl, gain, HG_ROW_HEADS, HG_HEADS - HG_ROW_HEADS, True, rows, "hgrn_col_heads")

    fmat, gmat = _dft_tables(seq)
    emb = _filter_embedding(seq)
    kpad = (-emb.shape[1]) % LANES
    emb = jnp.pad(emb, ((0, 0), (0, kpad)))
    fw1 = jnp.pad(flt_w1[layer], ((0, kpad), (0, 0)))
    row = lambda a: a[layer].reshape(1, -1)
    kspec = _kspec(emb, fw1, row(flt_b1), row(flt_freq), flt_w2[layer], row(flt_b2), flt_w3[layer],
                   _filter_deltas(hy_width)[None, :], fmat, 256)
    y_lat = _hyena(p_lat, hg_cols, hy_conv_w[layer], hy_conv_b[layer].reshape(1, -1), kspec, row(hy_bias),
                   fmat, gmat, 256)

    x_mid = _wout(a_row.reshape(b * seq, -1), a_col.reshape(b * seq, -1), y_lat.reshape(b * seq, hy_width),
                  x2, g1, w_out_b, seq, 512)
    out = _mlp(x_mid, norm2_g[layer], sh2, sc2, g2, w_mlp1_b, w_mlp2_b, final_norm_g, seq, 512, 1024)
    return out.reshape(b, seq, d)
```
